```python
import math
import jax, jax.numpy as jnp
from jax import lax
import numpy as np

D_MODEL = 1024
BATCH = 8
SEQ = 4096
DEPTH = 2

GRID_W = 64
CTX_LEN = 256
MIX_WIDTH = 2 * D_MODEL
SSD_WIDTH = D_MODEL
SSD_HEAD_DIM = 64
SSD_HEADS = SSD_WIDTH // SSD_HEAD_DIM
SSD_GROUPS = 2
SSD_HPG = SSD_HEADS // SSD_GROUPS
SSD_STATE = 128
SSD_CHUNK = 128
CONV_K = 3
CONV_CH = SSD_WIDTH + 2 * SSD_GROUPS * SSD_STATE
GLA_HEADS = 4
GLA_V_WIDTH = MIX_WIDTH - SSD_WIDTH
GLA_K_WIDTH = GLA_V_WIDTH // 2
GLA_DK = GLA_K_WIDTH // GLA_HEADS
GLA_DV = GLA_V_WIDTH // GLA_HEADS
GLA_GATE_RANK = 16
GLA_GATE_NORM = 16.0
GLA_CHUNK = 64
D_FF = 4 * D_MODEL
EPS = 1e-6
IN_COLS = (2 * SSD_WIDTH + 2 * SSD_GROUPS * SSD_STATE + 2 * SSD_HEADS
           + 2 * GLA_K_WIDTH + 2 * GLA_V_WIDTH + 2 * GLA_GATE_RANK)

kernel_name = 'hybrid_ssd_gla_prefix_dit_block'


def _in_split_points():
    widths = [SSD_WIDTH, SSD_WIDTH, SSD_GROUPS * SSD_STATE, SSD_GROUPS * SSD_STATE,
              SSD_HEADS, SSD_HEADS, GLA_K_WIDTH, GLA_K_WIDTH, GLA_V_WIDTH, GLA_V_WIDTH,
              GLA_GATE_RANK, GLA_GATE_RANK]
    pts, acc = [], 0
    for w in widths[:-1]:
        acc += w
        pts.append(acc)
    return pts


def rms_norm(x, g):
    xf = x.astype(jnp.float32)
    y = xf * lax.rsqrt(jnp.mean(xf * xf, axis=-1, keepdims=True) + EPS)
    return (y * g.astype(jnp.float32)).astype(x.dtype)


def modulate(h, shift, scale):
    return h * (1 + scale) + shift


def dw_conv_grid(u, w, bias, rows, cols):
    b, L, ch = u.shape
    img = u.reshape(b, rows, cols, ch)
    out = lax.conv_general_dilated(img, w[:, :, None, :].astype(u.dtype), window_strides=(1, 1),
                                   padding='SAME', dimension_numbers=('NHWC', 'HWIO', 'NHWC'),
                                   feature_group_count=ch)
    return (out + bias.astype(u.dtype)).reshape(b, L, ch)


def ssd_chunked_scan(xh, dt, A, Bm, Cm, h0):
    f32 = jnp.float32
    b, L, G, HG, P = xh.shape
    N = Bm.shape[-1]
    Q = SSD_CHUNK
    nc = L // Q
    x = xh.astype(f32).reshape(b, nc, Q, G, HG, P)
    dt = dt.reshape(b, nc, Q, G, HG)
    Bc = Bm.astype(f32).reshape(b, nc, Q, G, N)
    Cc = Cm.astype(f32).reshape(b, nc, Q, G, N)
    a_cum = jnp.cumsum(dt * A, axis=2)
    xdt = x * dt[..., None]
    a_t = jnp.moveaxis(a_cum, 2, -1)
    diff = a_t[..., :, None] - a_t[..., None, :]
    lower = jnp.tril(jnp.ones((Q, Q), dtype=bool))
    decay = jnp.exp(jnp.where(lower, diff, -jnp.inf))
    cb = jnp.einsum('bcign,bcjgn->bcgij', Cc, Bc)
    y_diag = jnp.einsum('bcghij,bcjghp->bcighp', cb[:, :, :, None] * decay, xdt)
    decay_to_end = jnp.exp(a_cum[:, :, -1:] - a_cum)
    states = jnp.einsum('bcjgn,bcjghp->bcghpn', Bc, xdt * decay_to_end[..., None])
    chunk_decay = jnp.exp(a_cum[:, :, -1])

    def step(h, inp):
        dec, st = inp
        return dec[..., None, None] * h + st, h

    h_final, h_prev = lax.scan(step, h0, (jnp.moveaxis(chunk_decay, 1, 0), jnp.moveaxis(states, 1, 0)))
    h_prev = jnp.moveaxis(h_prev, 0, 1)
    y_off = jnp.einsum('bcign,bcghpn->bcighp', Cc, h_prev) * jnp.exp(a_cum)[..., None]
    return (y_diag + y_off).reshape(b, L, G, HG, P), h_final


def gla_chunked_scan(q, k, v, log_a, S0):
    b, L, H, DK = q.shape
    DV = v.shape[-1]
    C = GLA_CHUNK
    nc = L // C
    q = q.reshape(b, nc, C, H, DK)
    k = k.reshape(b, nc, C, H, DK)
    v = v.reshape(b, nc, C, H, DV)
    bcum = jnp.cumsum(log_a.reshape(b, nc, C, H, DK), axis=2)
    b_last = bcum[:, :, -1:]
    q_dec = q * jnp.exp(bcum)
    k_inv = k * jnp.exp(-bcum)
    k_end = k * jnp.exp(b_last - bcum)
    lower = jnp.tril(jnp.ones((C, C), dtype=bool))
    att = jnp.where(lower, jnp.einsum('bcihd,bcjhd->bchij', q_dec, k_inv), 0.0)
    o_intra = jnp.einsum('bchij,bcjhv->bcihv', att, v)
    U = jnp.einsum('bcjhd,bcjhv->bchdv', k_end, v)
    dec = jnp.exp(b_last[:, :, 0])

    def step(S, inp):
        a, u_ = inp
        return a[..., None] * S + u_, S

    S_T, S_prev = lax.scan(step, S0, (jnp.moveaxis(dec, 1, 0), jnp.moveaxis(U, 1, 0)))
    S_prev = jnp.moveaxis(S_prev, 0, 1)
    o_inter = jnp.einsum('bcihd,bchdv->bcihv', q_dec, S_prev)
    return (o_intra + o_inter).reshape(b, L, H, DV), S_T


def hybrid_mixer(u, rows, cols, init, w_in, conv_w, conv_b, dt_bias, a_log, d_skip,
                 ssd_norm_g, gla_w2, gla_b2, gla_norm_g):
    f32 = jnp.float32
    b, L, _ = u.shape
    flip = lambda t: jnp.flip(t, axis=1)
    proj = u @ w_in
    z, xs, Bm, Cm, dt_f, dt_b, q, k, v, r, ga_f, ga_b = jnp.split(proj, _in_split_points(), axis=-1)

    xbc = jax.nn.silu(dw_conv_grid(jnp.concatenate([xs, Bm, Cm], axis=-1), conv_w, conv_b, rows, cols))
    xs, Bm, Cm = jnp.split(xbc, [SSD_WIDTH, SSD_WIDTH + SSD_GROUPS * SSD_STATE], axis=-1)
    xh = xs.reshape(b, L, SSD_GROUPS, SSD_HPG, SSD_HEAD_DIM)
    Bm = Bm.reshape(b, L, SSD_GROUPS, SSD_STATE)
    Cm = Cm.reshape(b, L, SSD_GROUPS, SSD_STATE)

    def ssd_direction(d, dt_raw, xh_, B_, C_, h0):
        dt = jax.nn.softplus(dt_raw.astype(f32) + dt_bias[d].astype(f32)).reshape(b, L, SSD_GROUPS, SSD_HPG)
        A = -jnp.exp(a_log[d].astype(f32)).reshape(SSD_GROUPS, SSD_HPG)
        y, hT = ssd_chunked_scan(xh_, dt, A, B_, C_, h0)
        return y + d_skip[d].astype(f32).reshape(SSD_GROUPS, SSD_HPG, 1) * xh_.astype(f32), hT

    y_f, hs_f = ssd_direction(0, dt_f, xh, Bm, Cm, init[0])
    y_b, hs_b = ssd_direction(1, flip(dt_b), flip(xh), flip(Bm), flip(Cm), init[1])
    y = (y_f + flip(y_b)).reshape(b, L, SSD_WIDTH) * jax.nn.silu(z.astype(f32))
    y = rms_norm(y.reshape(b, L, SSD_GROUPS, SSD_WIDTH // SSD_GROUPS),
                 ssd_norm_g.reshape(SSD_GROUPS, SSD_WIDTH // SSD_GROUPS)).reshape(b, L, SSD_WIDTH)

    qh = q.astype(f32).reshape(b, L, GLA_HEADS, GLA_DK) * (GLA_DK ** -0.5)
    kh = k.astype(f32).reshape(b, L, GLA_HEADS, GLA_DK)
    vh = v.astype(f32).reshape(b, L, GLA_HEADS, GLA_DV)

    def gla_direction(d, ga, q_, k_, v_, S0):
        log_a = jax.nn.log_sigmoid((ga @ gla_w2[d] + gla_b2[d]).astype(f32)) / GLA_GATE_NORM
        return gla_chunked_scan(q_, k_, v_, log_a.reshape(b, L, GLA_HEADS, GLA_DK), S0)

    o_f, S_f = gla_direction(0, ga_f, qh, kh, vh, init[2])
    o_b, S_b = gla_direction(1, flip(ga_b), flip(qh), flip(kh), flip(vh), init[3])
    o = rms_norm(o_f + flip(o_b), gla_norm_g).reshape(b, L, GLA_V_WIDTH) * jax.nn.silu(r.astype(f32))

    heads = jnp.concatenate([y, o], axis=-1).astype(u.dtype)
    return heads, (hs_f, hs_b, S_f, S_b)


def sq_relu_mlp(h, w1, w2):
    return jnp.square(jax.nn.relu(h @ w1)) @ w2


def setup_inputs(seed: int = 0) -> dict:
    key = jax.random.key(seed)
    ks = jax.random.split(key, 24)
    D = D_MODEL
    f32 = jnp.float32

    def nrm(k, shape, scale):
        return jax.random.normal(k, shape, f32) * scale

    dt0 = jnp.exp(jax.random.uniform(ks[10], (DEPTH, 2, SSD_HEADS), f32)
                  * (math.log(0.1) - math.log(0.001)) + math.log(0.001))
    return {
        'x': nrm(ks[0], (BATCH, SEQ, D), 1.0),
        'c': nrm(ks[1], (BATCH, D), 1.0),
        'ctx': nrm(ks[2], (BATCH, CTX_LEN, D), 1.0),
        'c_ctx': nrm(ks[3], (D,), 1.0),
        'w_ada': nrm(ks[4], (DEPTH, D, 6 * D), 0.5 * D ** -0.5),
        'b_ada': nrm(ks[5], (DEPTH, 6 * D), 0.02),
        'norm1_g': 1.0 + nrm(ks[6], (DEPTH, D), 0.02),
        'w_in': nrm(ks[7], (DEPTH, D, IN_COLS), D ** -0.5),
        'conv_w': nrm(ks[8], (DEPTH, CONV_K, CONV_K, CONV_CH), 1.0 / CONV_K),
        'conv_b': nrm(ks[9], (DEPTH, CONV_CH), 0.02),
        'dt_bias': dt0 + jnp.log(-jnp.expm1(-dt0)),
        'a_log': jnp.log(jax.random.uniform(ks[11], (DEPTH, 2, SSD_HEADS), f32, minval=1.0, maxval=16.0)),
        'd_skip': 1.0 + nrm(ks[12], (DEPTH, 2, SSD_HEADS), 0.02),
        'ssd_norm_g': 1.0 + nrm(ks[13], (DEPTH, SSD_WIDTH), 0.02),
        'gla_w2': nrm(ks[14], (DEPTH, 2, GLA_GATE_RANK, GLA_K_WIDTH), GLA_GATE_RANK ** -0.5),
        'gla_b2': nrm(ks[15], (DEPTH, 2, GLA_K_WIDTH), 0.02),
        'gla_norm_g': 1.0 + nrm(ks[16], (DEPTH, GLA_DV), 0.02),
        'w_out': nrm(ks[17], (DEPTH, MIX_WIDTH, D), MIX_WIDTH ** -0.5),
        'norm2_g': 1.0 + nrm(ks[18], (DEPTH, D), 0.02),
        'w_ff1': nrm(ks[19], (DEPTH, D, D_FF), D ** -0.5),
        'w_ff2': nrm(ks[20], (DEPTH, D_FF, D), D_FF ** -0.5),
        'final_norm_g': 1.0 + nrm(ks[21], (D,), 0.02),
    }


def reference(x, c, ctx, c_ctx, w_ada, b_ada, norm1_g, w_in, conv_w, conv_b, dt_bias, a_log,
              d_skip, ssd_norm_g, gla_w2, gla_b2, gla_norm_g, w_out, norm2_g, w_ff1, w_ff2,
              final_norm_g):
    f32 = jnp.float32
    bsz, n_lat, _ = x.shape
    rows = n_lat // GRID_W
    ctx_len = ctx.shape[1]
    zero_states = (jnp.zeros((bsz, SSD_GROUPS, SSD_HPG, SSD_HEAD_DIM, SSD_STATE), f32),
                   jnp.zeros((bsz, SSD_GROUPS, SSD_HPG, SSD_HEAD_DIM, SSD_STATE), f32),
                   jnp.zeros((bsz, GLA_HEADS, GLA_DK, GLA_DV), f32),
                   jnp.zeros((bsz, GLA_HEADS, GLA_DK, GLA_DV), f32))
    h_lat, h_ctx = x, ctx
    for l in range(DEPTH):
        mix_params = (w_in[l], conv_w[l], conv_b[l], dt_bias[l], a_log[l], d_skip[l],
                      ssd_norm_g[l], gla_w2[l], gla_b2[l], gla_norm_g[l])
        m_lat = jnp.split(jax.nn.silu(c) @ w_ada[l] + b_ada[l], 6, axis=-1)
        sh1, sc1, g1, sh2, sc2, g2 = [m[:, None, :] for m in m_lat]
        csh1, csc1, cg1, csh2, csc2, cg2 = jnp.split(jax.nn.silu(c_ctx) @ w_ada[l] + b_ada[l], 6, axis=-1)

        u_ctx = modulate(rms_norm(h_ctx, norm1_g[l]), csh1, csc1)
        heads_ctx, ctx_states = hybrid_mixer(u_ctx, 1, ctx_len, zero_states, *mix_params)

        u_lat = modulate(rms_norm(h_lat, norm1_g[l]), sh1, sc1)
        heads_lat, _ = hybrid_mixer(u_lat, rows, GRID_W, ctx_states, *mix_params)
        h_lat = h_lat + g1 * (heads_lat @ w_out[l])
        h_lat = h_lat + g2 * sq_relu_mlp(modulate(rms_norm(h_lat, norm2_g[l]), sh2, sc2), w_ff1[l], w_ff2[l])

        if l < DEPTH - 1:
            h_ctx = h_ctx + cg1 * (heads_ctx @ w_out[l])
            h_ctx = h_ctx + cg2 * sq_relu_mlp(modulate(rms_norm(h_ctx, norm2_g[l]), csh2, csc2), w_ff1[l], w_ff2[l])
    return rms_norm(h_lat, final_norm_g)
```

```python
import functools

import jax
import jax.numpy as jnp
from jax import lax
from jax.experimental import pallas as pl
from jax.experimental.pallas import tpu as pltpu

f32 = jnp.float32
bf16 = jnp.bfloat16

GRID_W = 64
SSD_HEAD_DIM = 64
SSD_GROUPS = 2
SSD_STATE = 128
SSD_CHUNK = 128
CONV_K = 3
GLA_HEADS = 4
GLA_GATE_RANK = 16
GLA_GATE_NORM = 16.0
GLA_CHUNK = 64
EPS = 1e-6

TILE = 256
LANES = 128
VMEM_LIMIT = 56 * 1024 * 1024


def _cparams(*sem):
    return pltpu.CompilerParams(dimension_semantics=sem, vmem_limit_bytes=VMEM_LIMIT)


def _softplus(x):
    return jnp.maximum(x, 0.0) + jnp.log1p(jnp.exp(-jnp.abs(x)))


def _silu(x):
    return x * jax.nn.sigmoid(x)


def _split_dot(lhs_bf16, x):
    hi = x.astype(bf16)
    lo = (x - hi.astype(f32)).astype(bf16)
    return (jnp.dot(lhs_bf16, hi, preferred_element_type=f32)
            + jnp.dot(lhs_bf16, lo, preferred_element_type=f32))


def _dot_nt(a, b):
    return lax.dot_general(a, b, (((1,), (1,)), ((), ())), preferred_element_type=f32)


def _dot_tn(a, b):
    return lax.dot_general(a, b, (((0,), (0,)), ((), ())), preferred_element_type=f32)


def _ada_kernel(cc_ref, w_ref, b_ref, o_ref):
    s = _silu(cc_ref[...]).astype(bf16)
    o_ref[0] = jnp.dot(s, w_ref[0].astype(bf16), preferred_element_type=f32) + b_ref[0]


def _ada(cc, w_ada, b_ada):
    depth, d, n = w_ada.shape
    tn = n // 4
    rows = cc.shape[0]
    return pl.pallas_call(
        _ada_kernel,
        grid=(depth, n // tn),
        in_specs=[pl.BlockSpec((rows, d), lambda l, j: (0, 0)),
                  pl.BlockSpec((1, d, tn), lambda l, j: (l, 0, j)),
                  pl.BlockSpec((1, 1, tn), lambda l, j: (l, 0, j))],
        out_specs=pl.BlockSpec((1, rows, tn), lambda l, j: (l, 0, j)),
        out_shape=jax.ShapeDtypeStruct((depth, rows, n), f32),
        compiler_params=_cparams("arbitrary", "arbitrary"),
        name="ada",
    )(cc, w_ada, b_ada.reshape(depth, 1, n))


def _inproj_kernel(h_ref, mod_ref, g_ref, w_ref, z_ref, xbc_ref, qk_ref, v_ref, r_ref, sm_ref):
    x = h_ref[0]
    m = mod_ref[0]
    ms = jnp.mean(x * x, axis=-1, keepdims=True)
    u = (x * lax.rsqrt(ms + EPS) * (g_ref[...] * (1.0 + m[1:2])) + m[0:1]).astype(bf16)
    col = 0
    for ref in (z_ref, xbc_ref, qk_ref, v_ref, r_ref, sm_ref):
        width = ref.shape[-1]
        ref[0] = jnp.dot(u, w_ref[:, col:col + width], preferred_element_type=f32).astype(ref.dtype)
        col += width


def _inproj(h, mod, g, w, widths, ctx_row):
    bsz, t, d = h.shape
    nt = t // TILE
    tok = lambda b, j: (b, j, 0)
    out_shape = [jax.ShapeDtypeStruct((bsz, t, wd), bf16) for wd in widths[:-1]]
    out_shape.append(jax.ShapeDtypeStruct((bsz, t, widths[-1]), f32))
    return pl.pallas_call(
        _inproj_kernel,
        grid=(bsz, nt),
        in_specs=[pl.BlockSpec((1, TILE, d), tok),
                  pl.BlockSpec((1, 6, d), lambda b, j: (jnp.where(j == 0, ctx_row, b), 0, 0)),
                  pl.BlockSpec((1, d), lambda b, j: (0, 0)),
                  pl.BlockSpec(w.shape, lambda b, j: (0, 0))],
        out_specs=[pl.BlockSpec((1, TILE, wd), tok) for wd in widths],
        out_shape=out_shape,
        compiler_params=_cparams("parallel", "arbitrary"),
        name="inproj",
    )(h, mod, g, w)


def _conv_kernel(x_ref, w_ref, b_ref, o_ref, *, ctx_len, rows, cols):
    cb = x_ref.shape[-1]
    w = w_ref[...]
    bias = b_ref[...]

    def taps(slab, i, n):
        col = lax.broadcasted_iota(jnp.int32, (n, cb), 0)
        left = jnp.where(col >= 1, pltpu.roll(slab, 1, 0), 0.0)
        right = jnp.where(col <= n - 2, pltpu.roll(slab, n - 1, 0), 0.0)
        return left * w[3 * i:3 * i + 1] + slab * w[3 * i + 1:3 * i + 2] + right * w[3 * i + 2:3 * i + 3]

    cslab = x_ref[0, 0:ctx_len, :].astype(f32)
    o_ref[0, 0:ctx_len, :] = _silu(bias + taps(cslab, 1, ctx_len)).astype(o_ref.dtype)

    def load_row(r):
        return x_ref[0, pl.ds(pl.multiple_of(ctx_len + r * cols, cols), cols), :].astype(f32)

    def body(r, carry):
        up = load_row(jnp.maximum(r - 1, 0)) * (r > 0).astype(f32)
        mid = load_row(r)
        down = load_row(jnp.minimum(r + 1, rows - 1)) * (r < rows - 1).astype(f32)
        acc = bias + taps(up, 0, cols) + taps(mid, 1, cols) + taps(down, 2, cols)
        o_ref[0, pl.ds(pl.multiple_of(ctx_len + r * cols, cols), cols), :] = _silu(acc).astype(o_ref.dtype)
        return carry

    lax.fori_loop(0, rows, body, 0)


def _conv(xbc, w9, bias, ctx_len, rows, cols):
    bsz, t, ch = xbc.shape
    cb = 256
    blk = pl.BlockSpec((1, t, cb), lambda b, c: (b, 0, c))
    return pl.pallas_call(
        functools.partial(_conv_kernel, ctx_len=ctx_len, rows=rows, cols=cols),
        grid=(bsz, ch // cb),
        in_specs=[blk,
                  pl.BlockSpec((CONV_K * CONV_K, cb), lambda b, c: (0, c)),
                  pl.BlockSpec((1, cb), lambda b, c: (0, c))],
        out_specs=blk,
        out_shape=jax.ShapeDtypeStruct(xbc.shape, bf16),
        compiler_params=_cparams("parallel", "arbitrary"),
        name="conv",
    )(xbc, w9, bias)


def _ssd_kernel(*refs, reverse, lane_off, final):
    if final:
        (x_ref, b_ref, c_ref, sm_ref, pv_ref, dexp_ref, ee_ref, z_ref, yf_ref, ng_ref,
         o_ref, state_ref, ybuf_ref) = refs
    else:
        x_ref, b_ref, c_ref, sm_ref, pv_ref, dexp_ref, ee_ref, o_ref, state_ref, ybuf_ref = refs
    q = SSD_CHUNK
    n = SSD_STATE
    gw = state_ref.shape[-1]
    pairs_per_group = gw // LANES
    hd = SSD_HEAD_DIM

    @pl.when(pl.program_id(1) == 0)
    def _():
        state_ref[...] = jnp.zeros_like(state_ref)

    ii = lax.broadcasted_iota(jnp.int32, (q, q), 0)
    jj = lax.broadcasted_iota(jnp.int32, (q, q), 1)
    mask = (jj >= ii) if reverse else (jj <= ii)
    tri = mask.astype(bf16)
    lane = lax.broadcasted_iota(jnp.int32, (q, LANES), 1)
    lo_half = lane < hd
    bias = pv_ref[0:1, :]
    a_coef = -jnp.exp(pv_ref[1:2, :]) * pv_ref[2:3, :]
    ee = ee_ref[...]
    dexp = dexp_ref[...]

    chunks = range(TILE // q)
    for ci in (reversed(chunks) if reverse else chunks):
        rs = slice(ci * q, (ci + 1) * q)
        dt = _softplus(sm_ref[0, rs, :] + bias)
        ac = _split_dot(tri, dt * a_coef)
        ac_end = ac[0:1] if reverse else ac[q - 1:q]
        ac_t = ac.T
        dt_t = dt.T
        wgt = dt * jnp.exp(ac_end - ac)
        w_hi = wgt.astype(bf16)
        w_lo = (wgt - w_hi.astype(f32)).astype(bf16)
        wexp = jnp.dot(jnp.concatenate([w_hi, w_lo], axis=1), ee, preferred_element_type=f32)
        e_end = jnp.broadcast_to(jnp.exp(ac_end), (8, LANES))
        e_hi = e_end.astype(bf16)
        e_lo = (e_end - e_hi.astype(f32)).astype(bf16)
        e_end_exp = jnp.dot(jnp.concatenate([e_hi, e_lo], axis=1), ee, preferred_element_type=f32)[0:1]

        xs = x_ref[0, rs, :]
        xs32 = xs.astype(f32)
        xw = (xs32 * wexp).astype(bf16)
        for g in range(SSD_GROUPS):
            bg = b_ref[0, rs, g * n:(g + 1) * n]
            cg = c_ref[0, rs, g * n:(g + 1) * n]
            cg32 = cg.astype(f32)
            cbm = _dot_nt(cg, bg)
            s_g = state_ref[g]
            s_g16 = s_g.astype(bf16)
            for pp in range(pairs_per_group):
                p = g * pairs_per_group + pp
                pieces = []
                for h in (2 * p, 2 * p + 1):
                    li = lane_off + h
                    colb = jnp.broadcast_to(ac[:, li:li + 1], (q, q))
                    dec = jnp.exp(jnp.where(mask, colb - ac_t[li:li + 1, :], -jnp.inf))
                    pieces.append((cbm * dec * dt_t[li:li + 1, :]).astype(bf16))
                    pieces.append((cg32 * jnp.exp(colb)).astype(bf16))
                lhs = jnp.concatenate(pieces, axis=1)
                xp = xs[:, p * LANES:(p + 1) * LANES]
                sp = s_g16[:, pp * LANES:(pp + 1) * LANES]
                zero = jnp.zeros_like(xp)
                rhs = jnp.concatenate([jnp.where(lo_half, xp, zero), jnp.where(lo_half, sp, zero),
                                       jnp.where(lo_half, zero, xp), jnp.where(lo_half, zero, sp)], axis=0)
                ybuf_ref[:, p * LANES:(p + 1) * LANES] = jnp.dot(lhs, rhs, preferred_element_type=f32)
            upd = _dot_tn(bg, xw[:, g * gw:(g + 1) * gw])
            state_ref[g] = s_g * e_end_exp[:, g * gw:(g + 1) * gw] + upd

        y = ybuf_ref[...] + dexp * xs32
        if final:
            zz = z_ref[0, rs, :].astype(f32)
            y = (y + yf_ref[0, rs, :].astype(f32)) * _silu(zz)
            for g in range(SSD_GROUPS):
                yg = y[:, g * gw:(g + 1) * gw]
                ms = jnp.mean(yg * yg, axis=-1, keepdims=True)
                o_ref[0, rs, g * gw:(g + 1) * gw] = (
                    yg * lax.rsqrt(ms + EPS) * ng_ref[:, g * gw:(g + 1) * gw]).astype(o_ref.dtype)
        else:
            o_ref[0, rs, :] = y.astype(o_ref.dtype)


def _scan_tile(nt, reverse):
    if reverse:
        return lambda s: jnp.where(s == 0, 0, nt - s)
    return lambda s: s


def _ssd(xbc, small, pvec, dexp, ee, width, reverse, lane_off, final_args=None):
    bsz, t, _ = xbc.shape
    nt = t // TILE
    tile = _scan_tile(nt, reverse)
    n = SSD_STATE
    gw = width // SSD_GROUPS
    tok = lambda cb: (lambda b, s: (b, tile(s), cb))
    const = lambda b, s: (0, 0)
    in_specs = [pl.BlockSpec((1, TILE, width), tok(0)),
                pl.BlockSpec((1, TILE, SSD_GROUPS * n), tok(width // (SSD_GROUPS * n))),
                pl.BlockSpec((1, TILE, SSD_GROUPS * n), tok(width // (SSD_GROUPS * n) + 1)),
                pl.BlockSpec((1, TILE, LANES), tok(0)),
                pl.BlockSpec(pvec.shape, const),
                pl.BlockSpec(dexp.shape, const),
                pl.BlockSpec(ee.shape, const)]
    args = [xbc, xbc, xbc, small, pvec, dexp, ee]
    if final_args is not None:
        z, yf, ng = final_args
        in_specs += [pl.BlockSpec((1, TILE, width), tok(0)),
                     pl.BlockSpec((1, TILE, width), tok(0)),
                     pl.BlockSpec(ng.shape, const)]
        args += [z, yf, ng]
    return pl.pallas_call(
        functools.partial(_ssd_kernel, reverse=reverse, lane_off=lane_off, final=final_args is not None),
        grid=(bsz, nt),
        in_specs=in_specs,
        out_specs=pl.BlockSpec((1, TILE, width), tok(0)),
        out_shape=jax.ShapeDtypeStruct((bsz, t, width), bf16),
        scratch_shapes=[pltpu.VMEM((SSD_GROUPS, n, gw), f32), pltpu.VMEM((SSD_CHUNK, width), f32)],
        compiler_params=_cparams("parallel", "arbitrary"),
        name="ssd_bwd" if reverse else "ssd_fwd",
    )(*args)


def _gla_kernel(*refs, reverse, final):
    if final:
        (q_ref, k_ref, v_ref, sm_ref, w2_ref, b2_ref, r_ref, of_ref, ng_ref,
         o_ref, state_ref, obuf_ref) = refs
    else:
        q_ref, k_ref, v_ref, sm_ref, w2_ref, b2_ref, o_ref, state_ref, obuf_ref = refs
    c = GLA_CHUNK
    dk = state_ref.shape[-1]
    dv = state_ref.shape[-2]
    scale = dk ** -0.5

    @pl.when(pl.program_id(1) == 0)
    def _():
        state_ref[...] = jnp.zeros_like(state_ref)

    ii = lax.broadcasted_iota(jnp.int32, (c, c), 0)
    jj = lax.broadcasted_iota(jnp.int32, (c, c), 1)
    mask = (jj >= ii) if reverse else (jj <= ii)
    tri = mask.astype(bf16)
    w2 = w2_ref[...]
    b2 = b2_ref[...]

    chunks = range(TILE // c)
    for ci in (reversed(chunks) if reverse else chunks):
        rs = slice(ci * c, (ci + 1) * c)
        gate = jnp.dot(sm_ref[0, rs, :].astype(bf16), w2, preferred_element_type=f32) + b2
        log_a = -_softplus(-gate) * (1.0 / GLA_GATE_NORM)
        bc = _split_dot(tri, log_a)
        b_end = bc[0:1] if reverse else bc[c - 1:c]
        q_dec = (q_ref[0, rs, :].astype(f32) * (jnp.exp(bc) * scale)).astype(bf16)
        k32 = k_ref[0, rs, :].astype(f32)
        k_inv = (k32 * jnp.exp(-bc)).astype(bf16)
        k_end = (k32 * jnp.exp(b_end - bc)).astype(bf16)
        dec = jnp.exp(b_end)
        for h in range(GLA_HEADS):
            ks = slice(h * dk, (h + 1) * dk)
            vs = slice(h * dv, (h + 1) * dv)
            v_h = v_ref[0, rs, vs]
            att = jnp.where(mask, _dot_nt(q_dec[:, ks], k_inv[:, ks]), 0.0).astype(bf16)
            s_t = state_ref[h]
            obuf_ref[:, vs] = (jnp.dot(att, v_h, preferred_element_type=f32)
                               + _dot_nt(q_dec[:, ks], s_t.astype(bf16)))
            state_ref[h] = s_t * dec[:, ks] + _dot_tn(v_h, k_end[:, ks])

        o = obuf_ref[...]
        if final:
            o = o + of_ref[0, rs, :].astype(f32)
            rr = r_ref[0, rs, :].astype(f32)
            for h in range(GLA_HEADS):
                vs = slice(h * dv, (h + 1) * dv)
                oh = o[:, vs]
                ms = jnp.mean(oh * oh, axis=-1, keepdims=True)
                o_ref[0, rs, vs] = (oh * lax.rsqrt(ms + EPS) * ng_ref[...] * _silu(rr[:, vs])).astype(o_ref.dtype)
        else:
            o_ref[0, rs, :] = o.astype(o_ref.dtype)


def _gla(qk, v, small, w2p, b2, reverse, final_args=None):
    bsz, t, kw2 = qk.shape
    kw = kw2 // 2
    vw = v.shape[-1]
    nt = t // TILE
    tile = _scan_tile(nt, reverse)
    tok = lambda cb: (lambda b, s: (b, tile(s), cb))
    const = lambda b, s: (0, 0)
    in_specs = [pl.BlockSpec((1, TILE, kw), tok(0)),
                pl.BlockSpec((1, TILE, kw), tok(1)),
                pl.BlockSpec((1, TILE, vw), tok(0)),
                pl.BlockSpec((1, TILE, LANES), tok(0)),
                pl.BlockSpec(w2p.shape, const),
                pl.BlockSpec(b2.shape, const)]
    args = [qk, qk, v, small, w2p, b2]
    if final_args is not None:
        r, of, ng = final_args
        in_specs += [pl.BlockSpec((1, TILE, vw), tok(0)),
                     pl.BlockSpec((1, TILE, vw), tok(0)),
                     pl.BlockSpec(ng.shape, const)]
        args += [r, of, ng]
    return pl.pallas_call(
        functools.partial(_gla_kernel, reverse=reverse, final=final_args is not None),
        grid=(bsz, nt),
        in_specs=in_specs,
        out_specs=pl.BlockSpec((1, TILE, vw), tok(0)),
        out_shape=jax.ShapeDtypeStruct((bsz, t, vw), bf16),
        scratch_shapes=[pltpu.VMEM((GLA_HEADS, vw // GLA_HEADS, kw // GLA_HEADS), f32),
                        pltpu.VMEM((GLA_CHUNK, vw), f32)],
        compiler_params=_cparams("parallel", "arbitrary"),
        name="gla_bwd" if reverse else "gla_fwd",
    )(*args)


def _outmlp_kernel(*refs, ff_chunk, final):
    if final:
        h_ref, ys_ref, og_ref, mod_ref, wo_ref, g2_ref, w1_ref, w2_ref, fg_ref, o_ref = refs
    else:
        h_ref, ys_ref, og_ref, mod_ref, wo_ref, g2_ref, w1_ref, w2_ref, o_ref = refs
    m = mod_ref[0]
    sw = ys_ref.shape[-1]
    mix = (jnp.dot(ys_ref[0], wo_ref[0:sw, :], preferred_element_type=f32)
           + jnp.dot(og_ref[0], wo_ref[sw:, :], preferred_element_type=f32))
    h1 = h_ref[0] + m[2:3] * mix
    ms = jnp.mean(h1 * h1, axis=-1, keepdims=True)
    u2 = (h1 * lax.rsqrt(ms + EPS) * (g2_ref[...] * (1.0 + m[4:5])) + m[3:4]).astype(bf16)
    acc = jnp.zeros_like(h1)
    for c0 in range(0, w1_ref.shape[-1], ff_chunk):
        hid = jnp.maximum(jnp.dot(u2, w1_ref[:, c0:c0 + ff_chunk], preferred_element_type=f32), 0.0)
        acc = acc + jnp.dot((hid * hid).astype(bf16), w2_ref[c0:c0 + ff_chunk, :], preferred_element_type=f32)
    h2 = h1 + m[5:6] * acc
    if final:
        ms2 = jnp.mean(h2 * h2, axis=-1, keepdims=True)
        h2 = h2 * lax.rsqrt(ms2 + EPS) * fg_ref[...]
    o_ref[0] = h2


def _outmlp(h, ys, og, mod, wo, g2, w1, w2, ctx_row, final_gain=None):
    bsz, t, d = h.shape
    nt = t // TILE
    final = final_gain is not None
    skip = 1 if final else 0
    tok = lambda b, j: (b, j + skip, 0)
    const = lambda b, j: (0, 0)
    resident = lambda a: pl.BlockSpec(a.shape, const, pipeline_mode=pl.Buffered(1))
    in_specs = [pl.BlockSpec((1, TILE, d), tok),
                pl.BlockSpec((1, TILE, ys.shape[-1]), tok),
                pl.BlockSpec((1, TILE, og.shape[-1]), tok),
                pl.BlockSpec((1, 6, d), lambda b, j: (jnp.where(j + skip == 0, ctx_row, b), 0, 0)),
                resident(wo), pl.BlockSpec((1, d), const), resident(w1), resident(w2)]
    args = [h, ys, og, mod, wo, g2, w1, w2]
    if final:
        in_specs.append(pl.BlockSpec((1, d), const))
        args.append(final_gain)
    return pl.pallas_call(
        functools.partial(_outmlp_kernel, ff_chunk=1024, final=final),
        grid=(bsz, nt - skip),
        in_specs=in_specs,
        out_specs=pl.BlockSpec((1, TILE, d), lambda b, j: (b, j, 0)),
        out_shape=jax.ShapeDtypeStruct((bsz, t - skip * TILE, d), f32),
        compiler_params=_cparams("parallel", "arbitrary"),
        name="outmlp_final" if final else "outmlp",
    )(*args)


def kernel(x, c, ctx, c_ctx, w_ada, b_ada, norm1_g, w_in, conv_w, conv_b, dt_bias, a_log, d_skip,
           ssd_norm_g, gla_w2, gla_b2, gla_norm_g, w_out, norm2_g, w_ff1, w_ff2, final_norm_g):
    bsz, n_lat, d = x.shape
    ctx_len = ctx.shape[1]
    depth = w_in.shape[0]
    ssd_w = ssd_norm_g.shape[-1]
    ssd_heads = dt_bias.shape[-1]
    gla_kw = gla_w2.shape[-1]
    gla_dv = gla_norm_g.shape[-1]
    gla_vw = GLA_HEADS * gla_dv
    bc_w = SSD_GROUPS * SSD_STATE
    rank = GLA_GATE_RANK
    assert ctx_len == TILE and n_lat % TILE == 0 and n_lat % GRID_W == 0
    assert ssd_w == ssd_heads * SSD_HEAD_DIM and 2 * ssd_heads + 2 * rank <= LANES
    assert w_in.shape[-1] == 2 * ssd_w + 2 * bc_w + 2 * ssd_heads + 2 * gla_kw + 2 * gla_vw + 2 * rank

    n_rows = -(-(bsz + 1) // 8) * 8
    cc = jnp.zeros((n_rows, d), f32).at[:bsz].set(c).at[bsz].set(c_ctx)
    mods = _ada(cc, w_ada, b_ada).reshape(depth, n_rows, 6, d)

    o_dt = 2 * ssd_w + 2 * bc_w
    o_q = o_dt + 2 * ssd_heads
    o_gate = o_q + 2 * gla_kw + 2 * gla_vw
    widths = (ssd_w, ssd_w + 2 * bc_w, 2 * gla_kw, gla_vw, gla_vw, LANES)
    n_small = 2 * ssd_heads + 2 * rank

    def expander(lane_off):
        lanes = jnp.arange(LANES)[:, None]
        heads = jnp.arange(ssd_w)[None, :] // SSD_HEAD_DIM
        e = (lanes == heads + lane_off).astype(bf16)
        return jnp.concatenate([e, e], axis=0)

    def lane_row(vals, off):
        return jnp.zeros((LANES,), f32).at[off:off + vals.shape[0]].set(vals)

    h = jnp.concatenate([ctx, x], axis=1)
    for l in range(depth):
        wl = w_in[l]
        w_cat = jnp.concatenate([wl[:, :o_dt], wl[:, o_q:o_gate], wl[:, o_dt:o_q], wl[:, o_gate:],
                                 jnp.zeros((d, LANES - n_small), f32)], axis=1).astype(bf16)
        z, xbc, qk, v, r, small = _inproj(h, mods[l], norm1_g[l][None], w_cat, widths, bsz)
        xbc = _conv(xbc, conv_w[l].reshape(CONV_K * CONV_K, -1), conv_b[l][None],
                    ctx_len, n_lat // GRID_W, GRID_W)

        ys = None
        for dirn in range(2):
            off = dirn * ssd_heads
            pvec = jnp.zeros((8, LANES), f32)
            pvec = pvec.at[0].set(lane_row(dt_bias[l, dirn], off)).at[1].set(lane_row(a_log[l, dirn], off))
            pvec = pvec.at[2].set(lane_row(jnp.ones((ssd_heads,), f32), off))
            dexp = jnp.repeat(d_skip[l, dirn], SSD_HEAD_DIM)[None]
            fin = None if dirn == 0 else (z, ys, ssd_norm_g[l][None])
            ys = _ssd(xbc, small, pvec, dexp, expander(off), ssd_w, dirn == 1, off, fin)

        og = None
        for dirn in range(2):
            off = 2 * ssd_heads + dirn * rank
            w2p = jnp.zeros((LANES, gla_kw), f32).at[off:off + rank].set(gla_w2[l, dirn]).astype(bf16)
            fin = None if dirn == 0 else (r, og, gla_norm_g[l][None])
            og = _gla(qk, v, small, w2p, gla_b2[l, dirn][None], dirn == 1, fin)

        last = l == depth - 1
        h = _outmlp(h, ys, og, mods[l], w_out[l].astype(bf16), norm2_g[l][None],
                    w_ff1[l].astype(bf16), w_ff2[l].astype(bf16), bsz,
                    final_norm_g[None] if last else None)
    return h
```

```python
import functools

import jax
import jax.numpy as jnp
from jax import lax
from jax.experimental import pallas as pl
from jax.experimental.pallas import tpu as pltpu

f32 = jnp.float32
bf16 = jnp.bfloat16

GRID_W = 64
SSD_HEAD_DIM = 64
SSD_GROUPS = 2
SSD_STATE = 128
SSD_CHUNK = 128
CONV_K = 3
GLA_HEADS = 4
GLA_GATE_RANK = 16
GLA_GATE_NORM = 16.0
GLA_CHUNK = 64
EPS = 1e-6

TILE = 256
LANES = 128
SUBLANES = 8
VMEM_LIMIT = 56 * 1024 * 1024
LOG2E = 1.4426950408889634


def _cparams(*sem):
    return pltpu.CompilerParams(dimension_semantics=sem, vmem_limit_bytes=VMEM_LIMIT)


def _softplus(x):
    return jnp.maximum(x, 0.0) + jnp.log(1.0 + jnp.exp2(-LOG2E * jnp.abs(x)))


def _silu(x):
    return x / (1.0 + jnp.exp2(-LOG2E * x))


def _split_dot(lhs_bf16, x):
    hi = x.astype(bf16)
    lo = (x - hi.astype(f32)).astype(bf16)
    return (jnp.dot(lhs_bf16, hi, preferred_element_type=f32)
            + jnp.dot(lhs_bf16, lo, preferred_element_type=f32))


def _dot_nt(a, b):
    return lax.dot_general(a, b, (((1,), (1,)), ((), ())), preferred_element_type=f32)


def _dot_tn(a, b):
    return lax.dot_general(a, b, (((0,), (0,)), ((), ())), preferred_element_type=f32)


def _chunk_masks(chunk, reverse):
    ii = lax.broadcasted_iota(jnp.int32, (TILE, TILE), 0)
    jj = lax.broadcasted_iota(jnp.int32, (TILE, TILE), 1)
    same = (ii // chunk) == (jj // chunk)
    return same & ((jj >= ii) if reverse else (jj <= ii))


def _chunk_ends(cum, chunk, reverse):
    n = TILE // chunk
    return [cum[ci * chunk:ci * chunk + 1] if reverse else cum[(ci + 1) * chunk - 1:(ci + 1) * chunk]
            for ci in range(n)]


def _ada_kernel(cc_ref, w_ref, b_ref, o_ref):
    s = _silu(cc_ref[...]).astype(bf16)
    o_ref[0] = jnp.dot(s, w_ref[0].astype(bf16), preferred_element_type=f32) + b_ref[0]


def _ada(cc, w_ada, b_ada):
    depth, d, n = w_ada.shape
    tn = n // 4
    rows = cc.shape[0]
    return pl.pallas_call(
        _ada_kernel,
        grid=(depth, n // tn),
        in_specs=[pl.BlockSpec((rows, d), lambda l, j: (0, 0)),
                  pl.BlockSpec((1, d, tn), lambda l, j: (l, 0, j)),
                  pl.BlockSpec((1, 1, tn), lambda l, j: (l, 0, j))],
        out_specs=pl.BlockSpec((1, rows, tn), lambda l, j: (l, 0, j)),
        out_shape=jax.ShapeDtypeStruct((depth, rows, n), f32),
        compiler_params=_cparams("arbitrary", "arbitrary"),
        name="ada",
    )(cc, w_ada, b_ada.reshape(depth, 1, n))


def _stream_specs(h, d):
    if isinstance(h, tuple):
        ctx, x = h
        return ([pl.BlockSpec((1, TILE, d), lambda b, j: (b, 0, 0)),
                 pl.BlockSpec((1, TILE, d), lambda b, j: (b, jnp.maximum(j - 1, 0), 0))], [ctx, x])
    return [pl.BlockSpec((1, TILE, d), lambda b, j: (b, j, 0))], [h]


def _read_stream(refs):
    if len(refs) == 2:
        return jnp.where(pl.program_id(1) == 0, refs[0][0], refs[1][0])
    return refs[0][0]


def _inproj_kernel(*refs, n_stream):
    mod_ref, g_ref, w_ref = refs[n_stream:n_stream + 3]
    x = _read_stream(refs[:n_stream])
    m = mod_ref[0]
    ms = jnp.mean(x * x, axis=-1, keepdims=True)
    u = (x * lax.rsqrt(ms + EPS) * (g_ref[...] * (1.0 + m[1:2])) + m[0:1]).astype(bf16)
    col = 0
    for ref in refs[n_stream + 3:]:
        width = ref.shape[-1]
        ref[0] = jnp.dot(u, w_ref[:, col:col + width], preferred_element_type=f32).astype(ref.dtype)
        col += width


def _inproj(h, mod, g, w, widths, ctx_row, bsz, t):
    d = w.shape[0]
    nt = t // TILE
    tok = lambda b, j: (b, j, 0)
    specs, args = _stream_specs(h, d)
    out_shape = [jax.ShapeDtypeStruct((bsz, t, wd), bf16) for wd in widths[:-1]]
    out_shape.append(jax.ShapeDtypeStruct((bsz, t, widths[-1]), f32))
    return pl.pallas_call(
        functools.partial(_inproj_kernel, n_stream=len(args)),
        grid=(bsz, nt),
        in_specs=specs + [pl.BlockSpec((1, 6, d), lambda b, j: (jnp.where(j == 0, ctx_row, b), 0, 0)),
                          pl.BlockSpec((1, d), lambda b, j: (0, 0)),
                          pl.BlockSpec(w.shape, lambda b, j: (0, 0))],
        out_specs=[pl.BlockSpec((1, TILE, wd), tok) for wd in widths],
        out_shape=out_shape,
        compiler_params=_cparams("parallel", "arbitrary"),
        name="inproj",
    )(*args, mod, g, w)


def _conv_kernel(x_ref, w_ref, b_ref, o_ref, *, ctx_len, rows, cols):
    cb = x_ref.shape[-1]
    w = w_ref[...]
    bias = b_ref[...]
    sub = lax.broadcasted_iota(jnp.int32, (SUBLANES, cb), 0)

    def column_sum(left, mid, right):
        n = mid.shape[0]
        rl = pltpu.roll(left, 1, 0)
        rr = pltpu.roll(right, n - 1, 0)
        rl = jnp.concatenate([jnp.where(sub >= 1, rl[:SUBLANES], 0.0), rl[SUBLANES:]], axis=0)
        rr = jnp.concatenate([rr[:n - SUBLANES], jnp.where(sub <= SUBLANES - 2, rr[n - SUBLANES:], 0.0)], axis=0)
        return rl + mid + rr

    def conv_row(slabs):
        parts = []
        for j in range(CONV_K):
            acc = None
            for slab, i in slabs:
                term = slab * w[CONV_K * i + j:CONV_K * i + j + 1]
                acc = term if acc is None else acc + term
            parts.append(acc)
        return _silu(bias + column_sum(*parts)).astype(o_ref.dtype)

    o_ref[0, 0:ctx_len, :] = conv_row([(x_ref[0, 0:ctx_len, :].astype(f32), 1)])

    def row_slice(r):
        return pl.ds(pl.multiple_of(ctx_len + r * cols, cols), cols)

    def load_row(r):
        return x_ref[0, row_slice(r), :].astype(f32)

    def edge_row(r):
        slabs = [(load_row(r + i - 1), i) for i in range(CONV_K) if 0 <= r + i - 1 < rows]
        o_ref[0, ctx_len + r * cols:ctx_len + (r + 1) * cols, :] = conv_row(slabs)

    edge_row(0)
    if rows > 1:
        edge_row(rows - 1)

    def body(r, carry):
        o_ref[0, row_slice(r), :] = conv_row([(load_row(r + i - 1), i) for i in range(CONV_K)])
        return carry

    lax.fori_loop(1, rows - 1, body, 0)


def _conv(xbc, w9, bias, ctx_len, rows, cols):
    bsz, t, ch = xbc.shape
    cb = 256
    blk = pl.BlockSpec((1, t, cb), lambda b, c: (b, 0, c))
    return pl.pallas_call(
        functools.partial(_conv_kernel, ctx_len=ctx_len, rows=rows, cols=cols),
        grid=(bsz, ch // cb),
        in_specs=[blk,
                  pl.BlockSpec((CONV_K * CONV_K, cb), lambda b, c: (0, c)),
                  pl.BlockSpec((1, cb), lambda b, c: (0, c))],
        out_specs=blk,
        out_shape=jax.ShapeDtypeStruct(xbc.shape, bf16),
        compiler_params=_cparams("parallel", "arbitrary"),
        name="conv",
    )(xbc, w9, bias)


def _ssd_kernel(*refs, reverse, lane_off, final):
    if final:
        (x_ref, b_ref, c_ref, sm_ref, pv_ref, ee_ref, z_ref, yf_ref, ng_ref, dexp_ref,
         o_ref, state_ref) = refs
    else:
        x_ref, b_ref, c_ref, sm_ref, pv_ref, ee_ref, o_ref, state_ref = refs
    q = SSD_CHUNK
    n = SSD_STATE
    gw = state_ref.shape[-1]
    pairs_per_group = gw // LANES
    hd = SSD_HEAD_DIM

    @pl.when(pl.program_id(1) == 0)
    def _():
        state_ref[...] = jnp.zeros_like(state_ref)

    tri = _chunk_masks(q, reverse).astype(bf16)
    ci_ = lax.broadcasted_iota(jnp.int32, (q, q), 0)
    cj_ = lax.broadcasted_iota(jnp.int32, (q, q), 1)
    mask = (cj_ >= ci_) if reverse else (cj_ <= ci_)
    lo_half = lax.broadcasted_iota(jnp.int32, (q, LANES), 1) < hd
    bias = pv_ref[0:1, :]
    a_coef = -jnp.exp(pv_ref[1:2, :]) * pv_ref[2:3, :] * LOG2E
    ee = ee_ref[...]

    def expand(v):
        hi = v.astype(bf16)
        lo = (v - hi.astype(f32)).astype(bf16)
        return jnp.dot(jnp.concatenate([hi, lo], axis=1), ee, preferred_element_type=f32)

    dt = _softplus(sm_ref[0] + bias)
    ac = _split_dot(tri, dt * a_coef)
    ends = _chunk_ends(ac, q, reverse)
    ac_end = jnp.concatenate([jnp.broadcast_to(e, (q, LANES)) for e in ends], axis=0)
    wexp = expand(dt * jnp.exp2(ac_end - ac)).astype(bf16)
    eexp = expand(jnp.exp2(ac))
    src_t = (ac - jnp.log2(dt)).T

    chunks = range(TILE // q)
    for ci in (reversed(chunks) if reverse else chunks):
        rs = slice(ci * q, (ci + 1) * q)
        end_row = ci * q if reverse else (ci + 1) * q - 1
        xs = x_ref[0, rs, :]
        xw = xs * wexp[rs]
        for g in range(SSD_GROUPS):
            gs = slice(g * gw, (g + 1) * gw)
            bg = b_ref[0, rs, g * n:(g + 1) * n]
            cg = c_ref[0, rs, g * n:(g + 1) * n]
            cbm = _dot_nt(cg, bg)
            s_g = state_ref[g]
            y_off = jnp.dot(cg, s_g.astype(bf16), preferred_element_type=f32) * eexp[rs, gs]
            ys = []
            for pp in range(pairs_per_group):
                p = g * pairs_per_group + pp
                pieces = []
                for h in (2 * p, 2 * p + 1):
                    li = lane_off + h
                    colb = jnp.broadcast_to(ac[rs, li:li + 1], (q, q))
                    dec = jnp.exp2(jnp.where(mask, colb - src_t[li:li + 1, rs], -jnp.inf))
                    pieces.append((cbm * dec).astype(bf16))
                lhs = jnp.concatenate(pieces, axis=1)
                xp = xs[:, p * LANES:(p + 1) * LANES]
                zero = jnp.zeros_like(xp)
                rhs = jnp.concatenate([jnp.where(lo_half, xp, zero), jnp.where(lo_half, zero, xp)], axis=0)
                ys.append(jnp.dot(lhs, rhs, preferred_element_type=f32))
            y = jnp.concatenate(ys, axis=1) + y_off
            state_ref[g] = s_g * eexp[end_row:end_row + 1, gs] + _dot_tn(bg, xw[:, gs])
            if final:
                y = y + yf_ref[0, rs, gs].astype(f32) + dexp_ref[:, gs] * xs[:, gs].astype(f32)
                y = y * _silu(z_ref[0, rs, gs].astype(f32))
                ms = jnp.mean(y * y, axis=-1, keepdims=True)
                y = y * lax.rsqrt(ms + EPS) * ng_ref[:, gs]
            o_ref[0, rs, gs] = y.astype(o_ref.dtype)


def _scan_tile(nt, reverse):
    if reverse:
        return lambda s: jnp.where(s == 0, 0, nt - s)
    return lambda s: s


def _ssd(xbc, small, pvec, ee, width, reverse, lane_off, final_args=None):
    bsz, t, _ = xbc.shape
    nt = t // TILE
    tile = _scan_tile(nt, reverse)
    n = SSD_STATE
    gw = width // SSD_GROUPS
    tok = lambda cb: (lambda b, s: (b, tile(s), cb))
    const = lambda b, s: (0, 0)
    in_specs = [pl.BlockSpec((1, TILE, width), tok(0)),
                pl.BlockSpec((1, TILE, SSD_GROUPS * n), tok(width // (SSD_GROUPS * n))),
                pl.BlockSpec((1, TILE, SSD_GROUPS * n), tok(width // (SSD_GROUPS * n) + 1)),
                pl.BlockSpec((1, TILE, LANES), tok(0)),
                pl.BlockSpec(pvec.shape, const),
                pl.BlockSpec(ee.shape, const)]
    args = [xbc, xbc, xbc, small, pvec, ee]
    if final_args is not None:
        z, yf, ng, dexp = final_args
        in_specs += [pl.BlockSpec((1, TILE, width), tok(0)),
                     pl.BlockSpec((1, TILE, width), tok(0)),
                     pl.BlockSpec(ng.shape, const),
                     pl.BlockSpec(dexp.shape, const)]
        args += [z, yf, ng, dexp]
    return pl.pallas_call(
        functools.partial(_ssd_kernel, reverse=reverse, lane_off=lane_off, final=final_args is not None),
        grid=(bsz, nt),
        in_specs=in_specs,
        out_specs=pl.BlockSpec((1, TILE, width), tok(0)),
        out_shape=jax.ShapeDtypeStruct((bsz, t, width), bf16),
        scratch_shapes=[pltpu.VMEM((SSD_GROUPS, n, gw), f32)],
        compiler_params=_cparams("parallel", "arbitrary"),
        name="ssd_bwd" if reverse else "ssd_fwd",
    )(*args)


def _gla_kernel(*refs, reverse, final):
    if final:
        q_ref, k_ref, v_ref, sm_ref, w2_ref, b2_ref, r_ref, of_ref, ng_ref, o_ref, state_ref = refs
    else:
        q_ref, k_ref, v_ref, sm_ref, w2_ref, b2_ref, o_ref, state_ref = refs
    c = GLA_CHUNK
    dk = state_ref.shape[1]
    dv = state_ref.shape[2]
    kw = GLA_HEADS * dk
    scale = dk ** -0.5

    @pl.when(pl.program_id(1) == 0)
    def _():
        state_ref[...] = jnp.zeros_like(state_ref)

    mask = _chunk_masks(c, reverse)
    tri = mask.astype(bf16)

    gate = jnp.dot(sm_ref[0].astype(bf16), w2_ref[...], preferred_element_type=f32) + b2_ref[...]
    log_a = -_softplus(-gate) * (LOG2E / GLA_GATE_NORM)
    bc = _split_dot(tri, log_a)
    ends = _chunk_ends(bc, c, reverse)
    b_end = jnp.concatenate([jnp.broadcast_to(e, (c, kw)) for e in ends], axis=0)
    q_dec = (q_ref[0].astype(f32) * (jnp.exp2(bc) * scale)).astype(bf16)
    k32 = k_ref[0].astype(f32)
    k_inv = (k32 * jnp.exp2(-bc)).astype(bf16)
    k_end = (k32 * jnp.exp2(b_end - bc)).astype(bf16)
    dec_cols = [jnp.broadcast_to(jnp.exp2(e), (LANES, kw)).T for e in ends]

    chunks = range(TILE // c)
    for h in range(GLA_HEADS):
        ks = slice(h * dk, (h + 1) * dk)
        vs = slice(h * dv, (h + 1) * dv)
        v_h = v_ref[0, :, vs]
        att = jnp.where(mask, _dot_nt(q_dec[:, ks], k_inv[:, ks]), 0.0).astype(bf16)
        o_intra = jnp.dot(att, v_h, preferred_element_type=f32)
        s = state_ref[h]
        outs = [None] * len(chunks)
        for ci in (reversed(chunks) if reverse else chunks):
            rs = slice(ci * c, (ci + 1) * c)
            outs[ci] = o_intra[rs] + jnp.dot(q_dec[rs, ks], s.astype(bf16), preferred_element_type=f32)
            dec = dec_cols[ci][ks, :]
            s = s * jnp.concatenate([dec] * (dv // LANES), axis=1) + _dot_tn(k_end[rs, ks], v_h[rs])
        state_ref[h] = s
        o = jnp.concatenate(outs, axis=0)
        if final:
            o = o + of_ref[0, :, vs].astype(f32)
            ms = jnp.mean(o * o, axis=-1, keepdims=True)
            o = o * lax.rsqrt(ms + EPS) * ng_ref[...] * _silu(r_ref[0, :, vs].astype(f32))
        o_ref[0, :, vs] = o.astype(o_ref.dtype)


def _gla(qk, v, small, w2p, b2, reverse, final_args=None):
    bsz, t, kw2 = qk.shape
    kw = kw2 // 2
    vw = v.shape[-1]
    nt = t // TILE
    tile = _scan_tile(nt, reverse)
    tok = lambda cb: (lambda b, s: (b, tile(s), cb))
    const = lambda b, s: (0, 0)
    in_specs = [pl.BlockSpec((1, TILE, kw), tok(0)),
                pl.BlockSpec((1, TILE, kw), tok(1)),
                pl.BlockSpec((1, TILE, vw), tok(0)),
                pl.BlockSpec((1, TILE, LANES), tok(0)),
                pl.BlockSpec(w2p.shape, const),
                pl.BlockSpec(b2.shape, const)]
    args = [qk, qk, v, small, w2p, b2]
    if final_args is not None:
        r, of, ng = final_args
        in_specs += [pl.BlockSpec((1, TILE, vw), tok(0)),
                     pl.BlockSpec((1, TILE, vw), tok(0)),
                     pl.BlockSpec(ng.shape, const)]
        args += [r, of, ng]
    return pl.pallas_call(
        functools.partial(_gla_kernel, reverse=reverse, final=final_args is not None),
        grid=(bsz, nt),
        in_specs=in_specs,
        out_specs=pl.BlockSpec((1, TILE, vw), tok(0)),
        out_shape=jax.ShapeDtypeStruct((bsz, t, vw), bf16),
        scratch_shapes=[pltpu.VMEM((GLA_HEADS, kw // GLA_HEADS, vw // GLA_HEADS), f32)],
        compiler_params=_cparams("parallel", "arbitrary"),
        name="gla_bwd" if reverse else "gla_fwd",
    )(*args)


def _outmlp_kernel(*refs, n_stream, ff_chunk, final):
    ys_ref, og_ref, mod_ref, wo_ref, g2_ref, w1_ref, w2_ref = refs[n_stream:n_stream + 7]
    o_ref = refs[-1]
    m = mod_ref[0]
    sw = ys_ref.shape[-1]
    mix = (jnp.dot(ys_ref[0], wo_ref[0:sw, :], preferred_element_type=f32)
           + jnp.dot(og_ref[0], wo_ref[sw:, :], preferred_element_type=f32))
    h1 = _read_stream(refs[:n_stream]) + m[2:3] * mix
    ms = jnp.mean(h1 * h1, axis=-1, keepdims=True)
    u2 = (h1 * lax.rsqrt(ms + EPS) * (g2_ref[...] * (1.0 + m[4:5])) + m[3:4]).astype(bf16)
    acc = jnp.zeros_like(h1)
    for c0 in range(0, w1_ref.shape[-1], ff_chunk):
        hid = jnp.maximum(jnp.dot(u2, w1_ref[:, c0:c0 + ff_chunk], preferred_element_type=f32), 0.0)
        acc = acc + jnp.dot((hid * hid).astype(bf16), w2_ref[c0:c0 + ff_chunk, :], preferred_element_type=f32)
    h2 = h1 + m[5:6] * acc
    if final:
        fg_ref = refs[n_stream + 7]
        ms2 = jnp.mean(h2 * h2, axis=-1, keepdims=True)
        h2 = h2 * lax.rsqrt(ms2 + EPS) * fg_ref[...]
    o_ref[0] = h2


def _outmlp(h, ys, og, mod, wo, g2, w1, w2, ctx_row, final_gain=None):
    bsz, t, _ = ys.shape
    d = wo.shape[-1]
    nt = t // TILE
    final = final_gain is not None
    skip = 1 if final else 0
    assert not (final and isinstance(h, tuple))
    tok = lambda b, j: (b, j + skip, 0)
    const = lambda b, j: (0, 0)
    resident = lambda a: pl.BlockSpec(a.shape, const, pipeline_mode=pl.Buffered(1))
    if isinstance(h, tuple):
        specs, args = _stream_specs(h, d)
    else:
        specs, args = [pl.BlockSpec((1, TILE, d), tok)], [h]
    n_stream = len(args)
    in_specs = specs + [pl.BlockSpec((1, TILE, ys.shape[-1]), tok),
                        pl.BlockSpec((1, TILE, og.shape[-1]), tok),
                        pl.BlockSpec((1, 6, d), lambda b, j: (jnp.where(j + skip == 0, ctx_row, b), 0, 0)),
                        resident(wo), pl.BlockSpec((1, d), const), resident(w1), resident(w2)]
    args = args + [ys, og, mod, wo, g2, w1, w2]
    if final:
        in_specs.append(pl.BlockSpec((1, d), const))
        args.append(final_gain)
    return pl.pallas_call(
        functools.partial(_outmlp_kernel, n_stream=n_stream, ff_chunk=1024, final=final),
        grid=(bsz, nt - skip),
        in_specs=in_specs,
        out_specs=pl.BlockSpec((1, TILE, d), lambda b, j: (b, j, 0)),
        out_shape=jax.ShapeDtypeStruct((bsz, t - skip * TILE, d), f32),
        compiler_params=_cparams("parallel", "arbitrary"),
        name="outmlp_final" if final else "outmlp",
    )(*args)


def kernel(x, c, ctx, c_ctx, w_ada, b_ada, norm1_g, w_in, conv_w, conv_b, dt_bias, a_log, d_skip,
           ssd_norm_g, gla_w2, gla_b2, gla_norm_g, w_out, norm2_g, w_ff1, w_ff2, final_norm_g):
    bsz, n_lat, d = x.shape
    ctx_len = ctx.shape[1]
    t = ctx_len + n_lat
    depth = w_in.shape[0]
    ssd_w = ssd_norm_g.shape[-1]
    ssd_heads = dt_bias.shape[-1]
    gla_kw = gla_w2.shape[-1]
    gla_dv = gla_norm_g.shape[-1]
    gla_vw = GLA_HEADS * gla_dv
    bc_w = SSD_GROUPS * SSD_STATE
    rank = GLA_GATE_RANK
    assert ctx_len == TILE and n_lat % TILE == 0 and n_lat % GRID_W == 0
    assert ssd_w == ssd_heads * SSD_HEAD_DIM and 2 * ssd_heads + 2 * rank <= LANES
    assert w_in.shape[-1] == 2 * ssd_w + 2 * bc_w + 2 * ssd_heads + 2 * gla_kw + 2 * gla_vw + 2 * rank

    n_rows = -(-(bsz + 1) // SUBLANES) * SUBLANES
    cc = jnp.zeros((n_rows, d), f32).at[:bsz].set(c).at[bsz].set(c_ctx)
    mods = _ada(cc, w_ada, b_ada).reshape(depth, n_rows, 6, d)

    o_dt = 2 * ssd_w + 2 * bc_w
    o_q = o_dt + 2 * ssd_heads
    o_gate = o_q + 2 * gla_kw + 2 * gla_vw
    widths = (ssd_w, ssd_w + 2 * bc_w, 2 * gla_kw, gla_vw, gla_vw, LANES)
    n_small = 2 * ssd_heads + 2 * rank
    w_in16 = w_in.astype(bf16)
    w_cat = jnp.concatenate([w_in16[:, :, :o_dt], w_in16[:, :, o_q:o_gate], w_in16[:, :, o_dt:o_q],
                             w_in16[:, :, o_gate:], jnp.zeros((depth, d, LANES - n_small), bf16)], axis=2)

    def expander(lane_off):
        lanes = jnp.arange(LANES)[:, None]
        heads = jnp.arange(ssd_w)[None, :] // SSD_HEAD_DIM
        e = (lanes == heads + lane_off).astype(bf16)
        return jnp.concatenate([e, e], axis=0)

    def lane_row(vals, off):
        return jnp.zeros((LANES,), f32).at[off:off + vals.shape[0]].set(vals)

    h = (ctx, x)
    for l in range(depth):
        z, xbc, qk, v, r, small = _inproj(h, mods[l], norm1_g[l][None], w_cat[l], widths, bsz, bsz, t)
        xbc = _conv(xbc, conv_w[l].reshape(CONV_K * CONV_K, -1), conv_b[l][None],
                    ctx_len, n_lat // GRID_W, GRID_W)

        ys = None
        for dirn in range(2):
            off = dirn * ssd_heads
            pvec = jnp.zeros((SUBLANES, LANES), f32)
            pvec = pvec.at[0].set(lane_row(dt_bias[l, dirn], off)).at[1].set(lane_row(a_log[l, dirn], off))
            pvec = pvec.at[2].set(lane_row(jnp.ones((ssd_heads,), f32), off))
            dexp = jnp.repeat(d_skip[l, 0] + d_skip[l, 1], SSD_HEAD_DIM)[None]
            fin = None if dirn == 0 else (z, ys, ssd_norm_g[l][None], dexp)
            ys = _ssd(xbc, small, pvec, expander(off), ssd_w, dirn == 1, off, fin)

        og = None
        for dirn in range(2):
            off = 2 * ssd_heads + dirn * rank
            w2p = jnp.zeros((LANES, gla_kw), f32).at[off:off + rank].set(gla_w2[l, dirn]).astype(bf16)
            fin = None if dirn == 0 else (r, og, gla_norm_g[l][None])
            og = _gla(qk, v, small, w2p, gla_b2[l, dirn][None], dirn == 1, fin)

        last = l == depth - 1
        h = _outmlp(h, ys, og, mods[l], w_out[l].astype(bf16), norm2_g[l][None],
                    w_ff1[l].astype(bf16), w_ff2[l].astype(bf16), bsz,
                    final_norm_g[None] if last else None)
    return h
```

```python
import functools

import jax
import jax.numpy as jnp
from jax import lax
from jax.experimental import pallas as pl
from jax.experimental.pallas import tpu as pltpu

f32 = jnp.float32
bf16 = jnp.bfloat16

GRID_W = 64
SSD_HEAD_DIM = 64
SSD_GROUPS = 2
SSD_STATE = 128
SSD_CHUNK = 128
CONV_K = 3
GLA_HEADS = 4
GLA_GATE_RANK = 16
GLA_GATE_NORM = 16.0
GLA_CHUNK = 64
EPS = 1e-6

TILE = 256
LANES = 128
SUBLANES = 8
VMEM_LIMIT = 56 * 1024 * 1024
LOG2E = 1.4426950408889634


def _cparams(*sem):
    return pltpu.CompilerParams(dimension_semantics=sem, vmem_limit_bytes=VMEM_LIMIT)


def _softplus(x):
    return jnp.maximum(x, 0.0) + jnp.log(1.0 + jnp.exp2(-LOG2E * jnp.abs(x)))


def _silu(x):
    return x / (1.0 + jnp.exp2(-LOG2E * x))


def _split_dot(lhs_bf16, x):
    hi = x.astype(bf16)
    lo = (x - hi.astype(f32)).astype(bf16)
    return (jnp.dot(lhs_bf16, hi, preferred_element_type=f32)
            + jnp.dot(lhs_bf16, lo, preferred_element_type=f32))


def _dot_nt(a, b):
    return lax.dot_general(a, b, (((1,), (1,)), ((), ())), preferred_element_type=f32)


def _dot_tn(a, b):
    return lax.dot_general(a, b, (((0,), (0,)), ((), ())), preferred_element_type=f32)


def _chunk_masks(chunk, reverse):
    ii = lax.broadcasted_iota(jnp.int32, (TILE, TILE), 0)
    jj = lax.broadcasted_iota(jnp.int32, (TILE, TILE), 1)
    same = (ii // chunk) == (jj // chunk)
    return same & ((jj >= ii) if reverse else (jj <= ii))


def _chunk_ends(cum, chunk, reverse):
    n = TILE // chunk
    return [cum[ci * chunk:ci * chunk + 1] if reverse else cum[(ci + 1) * chunk - 1:(ci + 1) * chunk]
            for ci in range(n)]


def _interleave(*streams):
    live = list(streams)
    while live:
        for s in list(live):
            if next(s, StopIteration) is StopIteration:
                live.remove(s)


def _ada_kernel(cc_ref, w_ref, b_ref, o_ref):
    s = _silu(cc_ref[...]).astype(bf16)
    o_ref[0] = jnp.dot(s, w_ref[0].astype(bf16), preferred_element_type=f32) + b_ref[0]


def _ada(cc, w_ada, b_ada):
    depth, d, n = w_ada.shape
    tn = n // 4
    rows = cc.shape[0]
    return pl.pallas_call(
        _ada_kernel,
        grid=(depth, n // tn),
        in_specs=[pl.BlockSpec((rows, d), lambda l, j: (0, 0)),
                  pl.BlockSpec((1, d, tn), lambda l, j: (l, 0, j)),
                  pl.BlockSpec((1, 1, tn), lambda l, j: (l, 0, j))],
        out_specs=pl.BlockSpec((1, rows, tn), lambda l, j: (l, 0, j)),
        out_shape=jax.ShapeDtypeStruct((depth, rows, n), f32),
        compiler_params=_cparams("arbitrary", "arbitrary"),
        name="ada",
    )(cc, w_ada, b_ada.reshape(depth, 1, n))


def _stream_specs(h, d):
    if isinstance(h, tuple):
        ctx, x = h
        return ([pl.BlockSpec((1, TILE, d), lambda b, j: (b, 0, 0)),
                 pl.BlockSpec((1, TILE, d), lambda b, j: (b, jnp.maximum(j - 1, 0), 0))], [ctx, x])
    return [pl.BlockSpec((1, TILE, d), lambda b, j: (b, j, 0))], [h]


def _read_stream(refs):
    if len(refs) == 2:
        return jnp.where(pl.program_id(1) == 0, refs[0][0], refs[1][0])
    return refs[0][0]


def _inproj_kernel(*refs, n_stream):
    mod_ref, g_ref, w_ref = refs[n_stream:n_stream + 3]
    x = _read_stream(refs[:n_stream])
    m = mod_ref[0]
    ms = jnp.mean(x * x, axis=-1, keepdims=True)
    u = (x * lax.rsqrt(ms + EPS) * (g_ref[...] * (1.0 + m[1:2])) + m[0:1]).astype(bf16)
    col = 0
    for ref in refs[n_stream + 3:]:
        width = ref.shape[-1]
        ref[0] = jnp.dot(u, w_ref[:, col:col + width], preferred_element_type=f32).astype(ref.dtype)
        col += width


def _inproj(h, mod, g, w, widths, ctx_row, bsz, t):
    d = w.shape[0]
    nt = t // TILE
    tok = lambda b, j: (b, j, 0)
    specs, args = _stream_specs(h, d)
    out_shape = [jax.ShapeDtypeStruct((bsz, t, wd), bf16) for wd in widths[:-1]]
    out_shape.append(jax.ShapeDtypeStruct((bsz, t, widths[-1]), f32))
    return pl.pallas_call(
        functools.partial(_inproj_kernel, n_stream=len(args)),
        grid=(bsz, nt),
        in_specs=specs + [pl.BlockSpec((1, 6, d), lambda b, j: (jnp.where(j == 0, ctx_row, b), 0, 0)),
                          pl.BlockSpec((1, d), lambda b, j: (0, 0)),
                          pl.BlockSpec(w.shape, lambda b, j: (0, 0))],
        out_specs=[pl.BlockSpec((1, TILE, wd), tok) for wd in widths],
        out_shape=out_shape,
        compiler_params=_cparams("parallel", "arbitrary"),
        name="inproj",
    )(*args, mod, g, w)


def _conv_kernel(x_ref, w_ref, b_ref, o_ref, *, ctx_len, rows, cols):
    cb = x_ref.shape[-1]
    w = w_ref[...]
    bias = b_ref[...]
    sub = lax.broadcasted_iota(jnp.int32, (SUBLANES, cb), 0)

    def column_sum(left, mid, right):
        n = mid.shape[0]
        rl = pltpu.roll(left, 1, 0)
        rr = pltpu.roll(right, n - 1, 0)
        rl = jnp.concatenate([jnp.where(sub >= 1, rl[:SUBLANES], 0.0), rl[SUBLANES:]], axis=0)
        rr = jnp.concatenate([rr[:n - SUBLANES], jnp.where(sub <= SUBLANES - 2, rr[n - SUBLANES:], 0.0)], axis=0)
        return rl + mid + rr

    def conv_row(slabs):
        parts = []
        for j in range(CONV_K):
            acc = None
            for slab, i in slabs:
                term = slab * w[CONV_K * i + j:CONV_K * i + j + 1]
                acc = term if acc is None else acc + term
            parts.append(acc)
        return _silu(bias + column_sum(*parts)).astype(o_ref.dtype)

    o_ref[0, 0:ctx_len, :] = conv_row([(x_ref[0, 0:ctx_len, :].astype(f32), 1)])

    def row_slice(r):
        return pl.ds(pl.multiple_of(ctx_len + r * cols, cols), cols)

    def load_row(r):
        return x_ref[0, row_slice(r), :].astype(f32)

    def edge_row(r):
        slabs = [(load_row(r + i - 1), i) for i in range(CONV_K) if 0 <= r + i - 1 < rows]
        o_ref[0, ctx_len + r * cols:ctx_len + (r + 1) * cols, :] = conv_row(slabs)

    edge_row(0)
    if rows > 1:
        edge_row(rows - 1)

    def body(r, carry):
        o_ref[0, row_slice(r), :] = conv_row([(load_row(r + i - 1), i) for i in range(CONV_K)])
        return carry

    lax.fori_loop(1, rows - 1, body, 0)


def _conv(xbc, w9, bias, ctx_len, rows, cols):
    bsz, t, ch = xbc.shape
    cb = 256
    blk = pl.BlockSpec((1, t, cb), lambda b, c: (b, 0, c))
    return pl.pallas_call(
        functools.partial(_conv_kernel, ctx_len=ctx_len, rows=rows, cols=cols),
        grid=(bsz, ch // cb),
        in_specs=[blk,
                  pl.BlockSpec((CONV_K * CONV_K, cb), lambda b, c: (0, c)),
                  pl.BlockSpec((1, cb), lambda b, c: (0, c))],
        out_specs=blk,
        out_shape=jax.ShapeDtypeStruct(xbc.shape, bf16),
        compiler_params=_cparams("parallel", "arbitrary"),
        name="conv",
    )(xbc, w9, bias)


def _ssd_tile(x_ref, b_ref, c_ref, sm_ref, pv_ref, ee_ref, o_ref, state_ref, *, reverse, lane_off):
    q = SSD_CHUNK
    n = SSD_STATE
    gw = state_ref.shape[-1]
    pairs_per_group = gw // LANES
    hd = SSD_HEAD_DIM

    tri = _chunk_masks(q, reverse).astype(bf16)
    ci_ = lax.broadcasted_iota(jnp.int32, (q, q), 0)
    cj_ = lax.broadcasted_iota(jnp.int32, (q, q), 1)
    mask = (cj_ >= ci_) if reverse else (cj_ <= ci_)
    lo_half = lax.broadcasted_iota(jnp.int32, (q, LANES), 1) < hd
    bias = pv_ref[0:1, :]
    a_coef = -jnp.exp(pv_ref[1:2, :]) * pv_ref[2:3, :] * LOG2E
    ee = ee_ref[...]

    def expand(v):
        hi = v.astype(bf16)
        lo = (v - hi.astype(f32)).astype(bf16)
        return jnp.dot(jnp.concatenate([hi, lo], axis=1), ee, preferred_element_type=f32)

    dt = _softplus(sm_ref[0] + bias)
    ac = _split_dot(tri, dt * a_coef)
    ends = _chunk_ends(ac, q, reverse)
    ac_end = jnp.concatenate([jnp.broadcast_to(e, (q, LANES)) for e in ends], axis=0)
    wexp = expand(dt * jnp.exp2(ac_end - ac)).astype(bf16)
    eexp = expand(jnp.exp2(ac))
    src_t = (ac - jnp.log2(dt)).T
    yield

    chunks = range(TILE // q)
    for ci in (reversed(chunks) if reverse else chunks):
        rs = slice(ci * q, (ci + 1) * q)
        end_row = ci * q if reverse else (ci + 1) * q - 1
        xs = x_ref[0, rs, :]
        xw = xs * wexp[rs]
        for g in range(SSD_GROUPS):
            gs = slice(g * gw, (g + 1) * gw)
            bg = b_ref[0, rs, g * n:(g + 1) * n]
            cg = c_ref[0, rs, g * n:(g + 1) * n]
            cbm = _dot_nt(cg, bg)
            s_g = state_ref[g]
            y_off = jnp.dot(cg, s_g.astype(bf16), preferred_element_type=f32) * eexp[rs, gs]
            ys = []
            for pp in range(pairs_per_group):
                p = g * pairs_per_group + pp
                pieces = []
                for h in (2 * p, 2 * p + 1):
                    li = lane_off + h
                    colb = jnp.broadcast_to(ac[rs, li:li + 1], (q, q))
                    dec = jnp.exp2(jnp.where(mask, colb - src_t[li:li + 1, rs], -jnp.inf))
                    pieces.append((cbm * dec).astype(bf16))
                lhs = jnp.concatenate(pieces, axis=1)
                xp = xs[:, p * LANES:(p + 1) * LANES]
                zero = jnp.zeros_like(xp)
                rhs = jnp.concatenate([jnp.where(lo_half, xp, zero), jnp.where(lo_half, zero, xp)], axis=0)
                ys.append(jnp.dot(lhs, rhs, preferred_element_type=f32))
                yield
            state_ref[g] = s_g * eexp[end_row:end_row + 1, gs] + _dot_tn(bg, xw[:, gs])
            o_ref[0, rs, gs] = (jnp.concatenate(ys, axis=1) + y_off).astype(o_ref.dtype)
            yield


def _ssd_kernel(xf_ref, bf_ref, cf_ref, smf_ref, xb_ref, bb_ref, cb_ref, smb_ref,
                pvf_ref, pvb_ref, eef_ref, eeb_ref, of_ref, ob_ref, state_ref, *, heads):
    @pl.when(pl.program_id(1) == 0)
    def _():
        state_ref[...] = jnp.zeros_like(state_ref)

    _interleave(_ssd_tile(xf_ref, bf_ref, cf_ref, smf_ref, pvf_ref, eef_ref, of_ref, state_ref.at[0],
                          reverse=False, lane_off=0),
                _ssd_tile(xb_ref, bb_ref, cb_ref, smb_ref, pvb_ref, eeb_ref, ob_ref, state_ref.at[1],
                          reverse=True, lane_off=heads))


def _scan_tile(nt, reverse):
    if reverse:
        return lambda s: jnp.where(s == 0, 0, nt - s)
    return lambda s: s


def _ssd(xbc, small, pvecs, ees, width, heads):
    bsz, t, _ = xbc.shape
    nt = t // TILE
    n = SSD_STATE
    gw = width // SSD_GROUPS
    bcb = width // (SSD_GROUPS * n)
    const = lambda b, s: (0, 0)
    in_specs, args = [], []
    for reverse in (False, True):
        tile = _scan_tile(nt, reverse)
        tok = lambda cb, tile=tile: (lambda b, s: (b, tile(s), cb))
        in_specs += [pl.BlockSpec((1, TILE, width), tok(0)),
                     pl.BlockSpec((1, TILE, SSD_GROUPS * n), tok(bcb)),
                     pl.BlockSpec((1, TILE, SSD_GROUPS * n), tok(bcb + 1)),
                     pl.BlockSpec((1, TILE, LANES), tok(0))]
        args += [xbc, xbc, xbc, small]
    in_specs += [pl.BlockSpec(a.shape, const) for a in (*pvecs, *ees)]
    args += [*pvecs, *ees]
    out_specs = [pl.BlockSpec((1, TILE, width), lambda b, s, tile=_scan_tile(nt, rev): (b, tile(s), 0))
                 for rev in (False, True)]
    return pl.pallas_call(
        functools.partial(_ssd_kernel, heads=heads),
        grid=(bsz, nt),
        in_specs=in_specs,
        out_specs=out_specs,
        out_shape=[jax.ShapeDtypeStruct((bsz, t, width), bf16)] * 2,
        scratch_shapes=[pltpu.VMEM((2, SSD_GROUPS, n, gw), f32)],
        compiler_params=_cparams("parallel", "arbitrary"),
        name="ssd",
    )(*args)


def _gla_tile(q_ref, k_ref, v_ref, sm_ref, w2_ref, b2_ref, o_ref, state_ref, *, reverse):
    c = GLA_CHUNK
    dk = state_ref.shape[1]
    dv = state_ref.shape[2]
    kw = GLA_HEADS * dk
    scale = dk ** -0.5

    mask = _chunk_masks(c, reverse)
    tri = mask.astype(bf16)

    gate = jnp.dot(sm_ref[0].astype(bf16), w2_ref[...], preferred_element_type=f32) + b2_ref[...]
    log_a = -_softplus(-gate) * (LOG2E / GLA_GATE_NORM)
    bc = _split_dot(tri, log_a)
    yield
    ends = _chunk_ends(bc, c, reverse)
    b_end = jnp.concatenate([jnp.broadcast_to(e, (c, kw)) for e in ends], axis=0)
    q_dec = (q_ref[0].astype(f32) * (jnp.exp2(bc) * scale)).astype(bf16)
    k32 = k_ref[0].astype(f32)
    k_inv = (k32 * jnp.exp2(-bc)).astype(bf16)
    k_end = (k32 * jnp.exp2(b_end - bc)).astype(bf16)
    dec_cols = [jnp.broadcast_to(jnp.exp2(e), (LANES, kw)).T for e in ends]
    yield

    chunks = range(TILE // c)
    for h in range(GLA_HEADS):
        ks = slice(h * dk, (h + 1) * dk)
        vs = slice(h * dv, (h + 1) * dv)
        v_h = v_ref[0, :, vs]
        att = jnp.where(mask, _dot_nt(q_dec[:, ks], k_inv[:, ks]), 0.0).astype(bf16)
        o_intra = jnp.dot(att, v_h, preferred_element_type=f32)
        yield
        s = state_ref[h]
        outs = [None] * len(chunks)
        for ci in (reversed(chunks) if reverse else chunks):
            rs = slice(ci * c, (ci + 1) * c)
            outs[ci] = o_intra[rs] + jnp.dot(q_dec[rs, ks], s.astype(bf16), preferred_element_type=f32)
            dec = dec_cols[ci][ks, :]
            s = s * jnp.concatenate([dec] * (dv // LANES), axis=1) + _dot_tn(k_end[rs, ks], v_h[rs])
            yield
        state_ref[h] = s
        o_ref[0, :, vs] = jnp.concatenate(outs, axis=0).astype(o_ref.dtype)


def _gla_kernel(qf_ref, kf_ref, vf_ref, smf_ref, qb_ref, kb_ref, vb_ref, smb_ref,
                w2f_ref, w2b_ref, b2f_ref, b2b_ref, of_ref, ob_ref, state_ref):
    @pl.when(pl.program_id(1) == 0)
    def _():
        state_ref[...] = jnp.zeros_like(state_ref)

    _interleave(_gla_tile(qf_ref, kf_ref, vf_ref, smf_ref, w2f_ref, b2f_ref, of_ref, state_ref.at[0], reverse=False),
                _gla_tile(qb_ref, kb_ref, vb_ref, smb_ref, w2b_ref, b2b_ref, ob_ref, state_ref.at[1], reverse=True))


def _gla(qk, v, small, w2ps, b2s):
    bsz, t, kw2 = qk.shape
    kw = kw2 // 2
    vw = v.shape[-1]
    nt = t // TILE
    const = lambda b, s: (0, 0)
    in_specs, args = [], []
    for reverse in (False, True):
        tile = _scan_tile(nt, reverse)
        tok = lambda cb, tile=tile: (lambda b, s: (b, tile(s), cb))
        in_specs += [pl.BlockSpec((1, TILE, kw), tok(0)),
                     pl.BlockSpec((1, TILE, kw), tok(1)),
                     pl.BlockSpec((1, TILE, vw), tok(0)),
                     pl.BlockSpec((1, TILE, LANES), tok(0))]
        args += [qk, qk, v, small]
    in_specs += [pl.BlockSpec(a.shape, const) for a in (*w2ps, *b2s)]
    args += [*w2ps, *b2s]
    out_specs = [pl.BlockSpec((1, TILE, vw), lambda b, s, tile=_scan_tile(nt, rev): (b, tile(s), 0))
                 for rev in (False, True)]
    return pl.pallas_call(
        _gla_kernel,
        grid=(bsz, nt),
        in_specs=in_specs,
        out_specs=out_specs,
        out_shape=[jax.ShapeDtypeStruct((bsz, t, vw), bf16)] * 2,
        scratch_shapes=[pltpu.VMEM((2, GLA_HEADS, kw // GLA_HEADS, vw // GLA_HEADS), f32)],
        compiler_params=_cparams("parallel", "arbitrary"),
        name="gla",
    )(*args)


def _outmlp_kernel(*refs, n_stream, ff_chunk, final):
    (yf_ref, yb_ref, xs_ref, z_ref, sg_ref, dexp_ref, of_ref, ob_ref, r_ref, gg_ref,
     mod_ref, wo_ref, g2_ref, w1_ref, w2_ref) = refs[n_stream:n_stream + 15]
    o_ref = refs[-1]
    m = mod_ref[0]

    sw = yf_ref.shape[-1]
    gw = sw // SSD_GROUPS
    y = (yf_ref[0].astype(f32) + yb_ref[0].astype(f32) + dexp_ref[...] * xs_ref[0].astype(f32))
    y = y * _silu(z_ref[0].astype(f32))
    mix = None
    for g in range(SSD_GROUPS):
        yg = y[:, g * gw:(g + 1) * gw]
        ms = jnp.mean(yg * yg, axis=-1, keepdims=True)
        part = (yg * lax.rsqrt(ms + EPS) * sg_ref[:, g * gw:(g + 1) * gw]).astype(bf16)
        term = jnp.dot(part, wo_ref[g * gw:(g + 1) * gw, :], preferred_element_type=f32)
        mix = term if mix is None else mix + term
    dv = gg_ref.shape[-1]
    o = of_ref[0].astype(f32) + ob_ref[0].astype(f32)
    gate = _silu(r_ref[0].astype(f32))
    for hh in range(GLA_HEADS):
        oh = o[:, hh * dv:(hh + 1) * dv]
        ms = jnp.mean(oh * oh, axis=-1, keepdims=True)
        part = (oh * lax.rsqrt(ms + EPS) * gg_ref[...] * gate[:, hh * dv:(hh + 1) * dv]).astype(bf16)
        mix = mix + jnp.dot(part, wo_ref[sw + hh * dv:sw + (hh + 1) * dv, :], preferred_element_type=f32)
    h1 = _read_stream(refs[:n_stream]) + m[2:3] * mix
    ms = jnp.mean(h1 * h1, axis=-1, keepdims=True)
    u2 = (h1 * lax.rsqrt(ms + EPS) * (g2_ref[...] * (1.0 + m[4:5])) + m[3:4]).astype(bf16)
    acc = jnp.zeros_like(h1)
    for c0 in range(0, w1_ref.shape[-1], ff_chunk):
        hid = jnp.maximum(jnp.dot(u2, w1_ref[:, c0:c0 + ff_chunk], preferred_element_type=f32), 0.0)
        acc = acc + jnp.dot((hid * hid).astype(bf16), w2_ref[c0:c0 + ff_chunk, :], preferred_element_type=f32)
    h2 = h1 + m[5:6] * acc
    if final:
        fg_ref = refs[n_stream + 15]
        ms2 = jnp.mean(h2 * h2, axis=-1, keepdims=True)
        h2 = h2 * lax.rsqrt(ms2 + EPS) * fg_ref[...]
    o_ref[0] = h2


def _outmlp(h, ssd_args, gla_args, mod, wo, g2, w1, w2, ctx_row, final_gain=None):
    yf, yb, xbc, z, sg, dexp = ssd_args
    of, ob, r, gg = gla_args
    bsz, t, sw = yf.shape
    vw = of.shape[-1]
    d = wo.shape[-1]
    nt = t // TILE
    final = final_gain is not None
    skip = 1 if final else 0
    assert not (final and isinstance(h, tuple))
    tok = lambda b, j: (b, j + skip, 0)
    const = lambda b, j: (0, 0)
    resident = lambda a: pl.BlockSpec(a.shape, const, pipeline_mode=pl.Buffered(1))
    if isinstance(h, tuple):
        specs, args = _stream_specs(h, d)
    else:
        specs, args = [pl.BlockSpec((1, TILE, d), tok)], [h]
    n_stream = len(args)
    in_specs = specs + [pl.BlockSpec((1, TILE, sw), tok)] * 4 + [pl.BlockSpec((1, sw), const)] * 2
    in_specs += [pl.BlockSpec((1, TILE, vw), tok)] * 3 + [pl.BlockSpec(gg.shape, const)]
    in_specs += [pl.BlockSpec((1, 6, d), lambda b, j: (jnp.where(j + skip == 0, ctx_row, b), 0, 0)),
                 resident(wo), pl.BlockSpec((1, d), const), resident(w1), resident(w2)]
    args = args + [yf, yb, xbc, z, sg, dexp, of, ob, r, gg, mod, wo, g2, w1, w2]
    if final:
        in_specs.append(pl.BlockSpec((1, d), const))
        args.append(final_gain)
    return pl.pallas_call(
        functools.partial(_outmlp_kernel, n_stream=n_stream, ff_chunk=1024, final=final),
        grid=(bsz, nt - skip),
        in_specs=in_specs,
        out_specs=pl.BlockSpec((1, TILE, d), lambda b, j: (b, j, 0)),
        out_shape=jax.ShapeDtypeStruct((bsz, t - skip * TILE, d), f32),
        compiler_params=_cparams("parallel", "arbitrary"),
        name="outmlp_final" if final else "outmlp",
    )(*args)


def kernel(x, c, ctx, c_ctx, w_ada, b_ada, norm1_g, w_in, conv_w, conv_b, dt_bias, a_log, d_skip,
           ssd_norm_g, gla_w2, gla_b2, gla_norm_g, w_out, norm2_g, w_ff1, w_ff2, final_norm_g):
    bsz, n_lat, d = x.shape
    ctx_len = ctx.shape[1]
    t = ctx_len + n_lat
    depth = w_in.shape[0]
    ssd_w = ssd_norm_g.shape[-1]
    ssd_heads = dt_bias.shape[-1]
    gla_kw = gla_w2.shape[-1]
    gla_dv = gla_norm_g.shape[-1]
    gla_vw = GLA_HEADS * gla_dv
    bc_w = SSD_GROUPS * SSD_STATE
    rank = GLA_GATE_RANK
    assert ctx_len == TILE and n_lat % TILE == 0 and n_lat % GRID_W == 0
    assert ssd_w == ssd_heads * SSD_HEAD_DIM and 2 * ssd_heads + 2 * rank <= LANES
    assert w_in.shape[-1] == 2 * ssd_w + 2 * bc_w + 2 * ssd_heads + 2 * gla_kw + 2 * gla_vw + 2 * rank

    n_rows = -(-(bsz + 1) // SUBLANES) * SUBLANES
    cc = jnp.zeros((n_rows, d), f32).at[:bsz].set(c).at[bsz].set(c_ctx)
    mods = _ada(cc, w_ada, b_ada).reshape(depth, n_rows, 6, d)

    o_dt = 2 * ssd_w + 2 * bc_w
    o_q = o_dt + 2 * ssd_heads
    o_gate = o_q + 2 * gla_kw + 2 * gla_vw
    widths = (ssd_w, ssd_w + 2 * bc_w, 2 * gla_kw, gla_vw, gla_vw, LANES)
    n_small = 2 * ssd_heads + 2 * rank
    w_in16 = w_in.astype(bf16)
    w_cat = jnp.concatenate([w_in16[:, :, :o_dt], w_in16[:, :, o_q:o_gate], w_in16[:, :, o_dt:o_q],
                             w_in16[:, :, o_gate:], jnp.zeros((depth, d, LANES - n_small), bf16)], axis=2)

    def expander(lane_off):
        lanes = jnp.arange(LANES)[:, None]
        heads = jnp.arange(ssd_w)[None, :] // SSD_HEAD_DIM
        e = (lanes == heads + lane_off).astype(bf16)
        return jnp.concatenate([e, e], axis=0)

    def lane_row(vals, off):
        return jnp.zeros((LANES,), f32).at[off:off + vals.shape[0]].set(vals)

    h = (ctx, x)
    for l in range(depth):
        z, xbc, qk, v, r, small = _inproj(h, mods[l], norm1_g[l][None], w_cat[l], widths, bsz, bsz, t)
        xbc = _conv(xbc, conv_w[l].reshape(CONV_K * CONV_K, -1), conv_b[l][None],
                    ctx_len, n_lat // GRID_W, GRID_W)

        pvecs, ees = [], []
        for dirn in range(2):
            off = dirn * ssd_heads
            pvec = jnp.zeros((SUBLANES, LANES), f32)
            pvec = pvec.at[0].set(lane_row(dt_bias[l, dirn], off)).at[1].set(lane_row(a_log[l, dirn], off))
            pvecs.append(pvec.at[2].set(lane_row(jnp.ones((ssd_heads,), f32), off)))
            ees.append(expander(off))
        y_f, y_b = _ssd(xbc, small, pvecs, ees, ssd_w, ssd_heads)

        w2ps = []
        for dirn in range(2):
            off = 2 * ssd_heads + dirn * rank
            w2ps.append(jnp.zeros((LANES, gla_kw), f32).at[off:off + rank].set(gla_w2[l, dirn]).astype(bf16))
        o_f, o_b = _gla(qk, v, small, w2ps, [gla_b2[l, 0][None], gla_b2[l, 1][None]])

        last = l == depth - 1
        dexp = jnp.repeat(d_skip[l, 0] + d_skip[l, 1], SSD_HEAD_DIM)[None]
        h = _outmlp(h, (y_f, y_b, xbc, z, ssd_norm_g[l][None], dexp), (o_f, o_b, r, gla_norm_g[l][None]),
                    mods[l], w_out[l].astype(bf16), norm2_g[l][None],
                    w_ff1[l].astype(bf16), w_ff2[l].astype(bf16), bsz,
                    final_norm_g[None] if last else None)
    return h
```

```python
import functools

import jax
import jax.numpy as jnp
from jax import lax
from jax.experimental import pallas as pl
from jax.experimental.pallas import tpu as pltpu

f32 = jnp.float32
bf16 = jnp.bfloat16

GRID_W = 64
SSD_HEAD_DIM = 64
SSD_GROUPS = 2
SSD_STATE = 128
SSD_CHUNK = 128
CONV_K = 3
GLA_HEADS = 4
GLA_GATE_RANK = 16
GLA_GATE_NORM = 16.0
GLA_CHUNK = 64
EPS = 1e-6

TILE = 256
LANES = 128
SUBLANES = 8
VMEM_LIMIT = 56 * 1024 * 1024
LOG2E = 1.4426950408889634


def _cparams(*sem):
    return pltpu.CompilerParams(dimension_semantics=sem, vmem_limit_bytes=VMEM_LIMIT)


def _softplus(x):
    return jnp.maximum(x, 0.0) + jnp.log(1.0 + jnp.exp2(-LOG2E * jnp.abs(x)))


def _silu(x):
    if x.dtype == bf16:
        return x / (1.0 + jnp.exp(-x))
    return x / (1.0 + jnp.exp2(-LOG2E * x))


def _split_dot(lhs_bf16, x):
    hi = x.astype(bf16)
    lo = (x - hi.astype(f32)).astype(bf16)
    return (jnp.dot(lhs_bf16, hi, preferred_element_type=f32)
            + jnp.dot(lhs_bf16, lo, preferred_element_type=f32))


def _dot_nt(a, b):
    return lax.dot_general(a, b, (((1,), (1,)), ((), ())), preferred_element_type=f32)


def _dot_tn(a, b):
    return lax.dot_general(a, b, (((0,), (0,)), ((), ())), preferred_element_type=f32)


def _chunk_masks(chunk, reverse):
    ii = lax.broadcasted_iota(jnp.int32, (TILE, TILE), 0)
    jj = lax.broadcasted_iota(jnp.int32, (TILE, TILE), 1)
    same = (ii // chunk) == (jj // chunk)
    return same & ((jj >= ii) if reverse else (jj <= ii))


def _chunk_ends(cum, chunk, reverse):
    n = TILE // chunk
    return [cum[ci * chunk:ci * chunk + 1] if reverse else cum[(ci + 1) * chunk - 1:(ci + 1) * chunk]
            for ci in range(n)]


def _interleave(*streams):
    live = list(streams)
    while live:
        for s in list(live):
            if next(s, StopIteration) is StopIteration:
                live.remove(s)


def _ada_kernel(cc_ref, w_ref, b_ref, o_ref):
    s = _silu(cc_ref[...]).astype(bf16)
    o_ref[0] = jnp.dot(s, w_ref[0].astype(bf16), preferred_element_type=f32) + b_ref[0]


def _ada(cc, w_ada, b_ada):
    depth, d, n = w_ada.shape
    tn = n // 4
    rows = cc.shape[0]
    return pl.pallas_call(
        _ada_kernel,
        grid=(depth, n // tn),
        in_specs=[pl.BlockSpec((rows, d), lambda l, j: (0, 0)),
                  pl.BlockSpec((1, d, tn), lambda l, j: (l, 0, j)),
                  pl.BlockSpec((1, 1, tn), lambda l, j: (l, 0, j))],
        out_specs=pl.BlockSpec((1, rows, tn), lambda l, j: (l, 0, j)),
        out_shape=jax.ShapeDtypeStruct((depth, rows, n), f32),
        compiler_params=_cparams("arbitrary", "arbitrary"),
        name="ada",
    )(cc, w_ada, b_ada.reshape(depth, 1, n))


def _stream_specs(h, d):
    if isinstance(h, tuple):
        ctx, x = h
        return ([pl.BlockSpec((1, TILE, d), lambda b, j: (b, 0, 0)),
                 pl.BlockSpec((1, TILE, d), lambda b, j: (b, jnp.maximum(j - 1, 0), 0))], [ctx, x])
    return [pl.BlockSpec((1, TILE, d), lambda b, j: (b, j, 0))], [h]


def _read_stream(refs):
    if len(refs) == 2:
        return jnp.where(pl.program_id(1) == 0, refs[0][0], refs[1][0])
    return refs[0][0]


def _inproj_kernel(*refs, n_stream, cols, nt):
    up_ref, dn_ref, mod_ref, g_ref, w_ref, cw_ref, cb_ref = refs[n_stream:n_stream + 7]
    z_ref, xbc_ref, qk_ref, v_ref, r_ref, sm_ref = refs[n_stream + 7:]
    j = pl.program_id(1)
    m = mod_ref[0]
    gain = g_ref[...] * (1.0 + m[1:2])

    def normed(x):
        ms = jnp.mean(x * x, axis=-1, keepdims=True)
        return (x * lax.rsqrt(ms + EPS) * gain + m[0:1]).astype(bf16)

    u = normed(_read_stream(refs[:n_stream]))
    u_ext = jnp.concatenate([normed(up_ref[0]), u, normed(dn_ref[0])], axis=0)

    zw = z_ref.shape[-1]
    cw = xbc_ref.shape[-1]
    xe = jnp.dot(u_ext, w_ref[:, zw:zw + cw], preferred_element_type=f32)

    is_ctx = j == 0
    lat = jnp.where(is_ctx, 0.0, 1.0)
    top = jnp.where(j <= 1, 0.0, 1.0)
    bot = jnp.where(is_ctx | (j == nt - 1), 0.0, 1.0)
    w = cw_ref[...]
    rows_per_tile = TILE // cols
    parts = []
    for kj in range(CONV_K):
        blocks = []
        for lr in range(rows_per_tile):
            f_up = top if lr == 0 else lat
            f_dn = bot if lr == rows_per_tile - 1 else lat
            blocks.append(xe[lr * cols:(lr + 1) * cols] * (w[kj:kj + 1] * f_up)
                          + xe[(lr + 1) * cols:(lr + 2) * cols] * w[CONV_K + kj:CONV_K + kj + 1]
                          + xe[(lr + 2) * cols:(lr + 3) * cols] * (w[2 * CONV_K + kj:2 * CONV_K + kj + 1] * f_dn))
        parts.append(jnp.concatenate(blocks, axis=0))

    sub = lax.broadcasted_iota(jnp.int32, (SUBLANES, cw), 0)
    rl = pltpu.roll(parts[0], 1, 0)
    rr = pltpu.roll(parts[2], TILE - 1, 0)
    out_blocks = []
    for t0 in range(0, TILE, SUBLANES):
        lb = rl[t0:t0 + SUBLANES]
        rb = rr[t0:t0 + SUBLANES]
        if t0 % cols == 0:
            keep = (sub >= 1) if t0 == 0 else ((sub >= 1) | is_ctx)
            lb = jnp.where(keep, lb, 0.0)
        if (t0 + SUBLANES) % cols == 0:
            keep = (sub <= SUBLANES - 2) if t0 + SUBLANES == TILE else ((sub <= SUBLANES - 2) | is_ctx)
            rb = jnp.where(keep, rb, 0.0)
        out_blocks.append(lb + rb)
    conv = jnp.concatenate(out_blocks, axis=0) + parts[1] + cb_ref[...]
    xbc_ref[0] = _silu(conv).astype(xbc_ref.dtype)

    col = 0
    for ref in (z_ref, xbc_ref, qk_ref, v_ref, r_ref, sm_ref):
        width = ref.shape[-1]
        if ref is not xbc_ref:
            ref[0] = jnp.dot(u, w_ref[:, col:col + width], preferred_element_type=f32).astype(ref.dtype)
        col += width


def _inproj(h, mod, g, w, cw9, cbias, widths, ctx_row, bsz, t, cols):
    d = w.shape[0]
    nt = t // TILE
    rpt = TILE // cols
    tok = lambda b, j: (b, j, 0)
    specs, args = _stream_specs(h, d)
    if isinstance(h, tuple):
        lat, first, last = h[1], 0, (t - TILE) // cols - 1
    else:
        lat, first, last = h, rpt, t // cols - 1
    up = lambda b, j: (b, jnp.clip(first + (j - 1) * rpt - 1, first, last), 0)
    dn = lambda b, j: (b, jnp.clip(first + j * rpt, first, last), 0)
    const = lambda b, j: (0, 0)
    out_shape = [jax.ShapeDtypeStruct((bsz, t, wd), bf16) for wd in widths[:-1]]
    out_shape.append(jax.ShapeDtypeStruct((bsz, t, widths[-1]), f32))
    return pl.pallas_call(
        functools.partial(_inproj_kernel, n_stream=len(args), cols=cols, nt=nt),
        grid=(bsz, nt),
        in_specs=specs + [pl.BlockSpec((1, cols, d), up), pl.BlockSpec((1, cols, d), dn),
                          pl.BlockSpec((1, 6, d), lambda b, j: (jnp.where(j == 0, ctx_row, b), 0, 0)),
                          pl.BlockSpec((1, d), const),
                          pl.BlockSpec(w.shape, const),
                          pl.BlockSpec(cw9.shape, const),
                          pl.BlockSpec(cbias.shape, const)],
        out_specs=[pl.BlockSpec((1, TILE, wd), tok) for wd in widths],
        out_shape=out_shape,
        compiler_params=_cparams("parallel", "arbitrary"),
        name="inproj",
    )(*args, lat, lat, mod, g, w, cw9, cbias)


def _ssd_tile(x_ref, b_ref, c_ref, sm_ref, pv_ref, ee_ref, o_ref, state_ref, *, reverse, lane_off):
    q = SSD_CHUNK
    n = SSD_STATE
    gw = state_ref.shape[-1]
    pairs_per_group = gw // LANES
    hd = SSD_HEAD_DIM

    tri = _chunk_masks(q, reverse).astype(bf16)
    ci_ = lax.broadcasted_iota(jnp.int32, (q, q), 0)
    cj_ = lax.broadcasted_iota(jnp.int32, (q, q), 1)
    mask = (cj_ >= ci_) if reverse else (cj_ <= ci_)
    lo_half = lax.broadcasted_iota(jnp.int32, (q, LANES), 1) < hd
    bias = pv_ref[0:1, :]
    a_coef = -jnp.exp(pv_ref[1:2, :]) * pv_ref[2:3, :] * LOG2E
    ee = ee_ref[...]

    def expand(v):
        hi = v.astype(bf16)
        lo = (v - hi.astype(f32)).astype(bf16)
        return jnp.dot(jnp.concatenate([hi, lo], axis=1), ee, preferred_element_type=f32)

    dt = _softplus(sm_ref[0] + bias)
    ac = _split_dot(tri, dt * a_coef)
    ends = _chunk_ends(ac, q, reverse)
    ac_end = jnp.concatenate([jnp.broadcast_to(e, (q, LANES)) for e in ends], axis=0)
    wexp = expand(dt * jnp.exp2(ac_end - ac)).astype(bf16)
    eexp = expand(jnp.exp2(ac))
    src_t = (ac - jnp.log2(dt)).T
    yield

    chunks = range(TILE // q)
    for ci in (reversed(chunks) if reverse else chunks):
        rs = slice(ci * q, (ci + 1) * q)
        end_row = ci * q if reverse else (ci + 1) * q - 1
        xs = x_ref[0, rs, :]
        xw = xs * wexp[rs]
        for g in range(SSD_GROUPS):
            gs = slice(g * gw, (g + 1) * gw)
            bg = b_ref[0, rs, g * n:(g + 1) * n]
            cg = c_ref[0, rs, g * n:(g + 1) * n]
            cbm = _dot_nt(cg, bg)
            s_g = state_ref[g]
            y_off = jnp.dot(cg, s_g.astype(bf16), preferred_element_type=f32) * eexp[rs, gs]
            ys = []
            for pp in range(pairs_per_group):
                p = g * pairs_per_group + pp
                pieces = []
                for h in (2 * p, 2 * p + 1):
                    li = lane_off + h
                    colb = jnp.broadcast_to(ac[rs, li:li + 1], (q, q))
                    dec = jnp.exp2(jnp.where(mask, colb - src_t[li:li + 1, rs], -jnp.inf))
                    pieces.append((cbm * dec).astype(bf16))
                lhs = jnp.concatenate(pieces, axis=1)
                xp = xs[:, p * LANES:(p + 1) * LANES]
                zero = jnp.zeros_like(xp)
                rhs = jnp.concatenate([jnp.where(lo_half, xp, zero), jnp.where(lo_half, zero, xp)], axis=0)
                ys.append(jnp.dot(lhs, rhs, preferred_element_type=f32))
                yield
            state_ref[g] = s_g * eexp[end_row:end_row + 1, gs] + _dot_tn(bg, xw[:, gs])
            o_ref[0, rs, gs] = (jnp.concatenate(ys, axis=1) + y_off).astype(o_ref.dtype)
            yield


def _ssd_kernel(xf_ref, bf_ref, cf_ref, smf_ref, xb_ref, bb_ref, cb_ref, smb_ref,
                pvf_ref, pvb_ref, eef_ref, eeb_ref, of_ref, ob_ref, state_ref, *, heads):
    @pl.when(pl.program_id(1) == 0)
    def _():
        state_ref[...] = jnp.zeros_like(state_ref)

    _interleave(_ssd_tile(xf_ref, bf_ref, cf_ref, smf_ref, pvf_ref, eef_ref, of_ref, state_ref.at[0],
                          reverse=False, lane_off=0),
                _ssd_tile(xb_ref, bb_ref, cb_ref, smb_ref, pvb_ref, eeb_ref, ob_ref, state_ref.at[1],
                          reverse=True, lane_off=heads))


def _scan_tile(nt, reverse):
    if reverse:
        return lambda s: jnp.where(s == 0, 0, nt - s)
    return lambda s: s


def _ssd(xbc, small, pvecs, ees, width, heads):
    bsz, t, _ = xbc.shape
    nt = t // TILE
    n = SSD_STATE
    gw = width // SSD_GROUPS
    bcb = width // (SSD_GROUPS * n)
    const = lambda b, s: (0, 0)
    in_specs, args = [], []
    for reverse in (False, True):
        tile = _scan_tile(nt, reverse)
        tok = lambda cb, tile=tile: (lambda b, s: (b, tile(s), cb))
        in_specs += [pl.BlockSpec((1, TILE, width), tok(0)),
                     pl.BlockSpec((1, TILE, SSD_GROUPS * n), tok(bcb)),
                     pl.BlockSpec((1, TILE, SSD_GROUPS * n), tok(bcb + 1)),
                     pl.BlockSpec((1, TILE, LANES), tok(0))]
        args += [xbc, xbc, xbc, small]
    in_specs += [pl.BlockSpec(a.shape, const) for a in (*pvecs, *ees)]
    args += [*pvecs, *ees]
    out_specs = [pl.BlockSpec((1, TILE, width), lambda b, s, tile=_scan_tile(nt, rev): (b, tile(s), 0))
                 for rev in (False, True)]
    return pl.pallas_call(
        functools.partial(_ssd_kernel, heads=heads),
        grid=(bsz, nt),
        in_specs=in_specs,
        out_specs=out_specs,
        out_shape=[jax.ShapeDtypeStruct((bsz, t, width), bf16)] * 2,
        scratch_shapes=[pltpu.VMEM((2, SSD_GROUPS, n, gw), f32)],
        compiler_params=_cparams("parallel", "arbitrary"),
        name="ssd",
    )(*args)


def _gla_tile(q_ref, k_ref, v_ref, sm_ref, w2_ref, b2_ref, o_ref, state_ref, *, reverse):
    c = GLA_CHUNK
    dk = state_ref.shape[1]
    dv = state_ref.shape[2]
    kw = GLA_HEADS * dk
    scale = dk ** -0.5

    mask = _chunk_masks(c, reverse)
    tri = mask.astype(bf16)

    gate = jnp.dot(sm_ref[0].astype(bf16), w2_ref[...], preferred_element_type=f32) + b2_ref[...]
    log_a = -_softplus(-gate) * (LOG2E / GLA_GATE_NORM)
    bc = _split_dot(tri, log_a)
    yield
    ends = _chunk_ends(bc, c, reverse)
    b_end = jnp.concatenate([jnp.broadcast_to(e, (c, kw)) for e in ends], axis=0)
    q_dec = (q_ref[0].astype(f32) * (jnp.exp2(bc) * scale)).astype(bf16)
    k32 = k_ref[0].astype(f32)
    k_inv = (k32 * jnp.exp2(-bc)).astype(bf16)
    k_end = (k32 * jnp.exp2(b_end - bc)).astype(bf16)
    dec_cols = [jnp.broadcast_to(jnp.exp2(e), (LANES, kw)).T for e in ends]
    yield

    chunks = range(TILE // c)
    for h in range(GLA_HEADS):
        ks = slice(h * dk, (h + 1) * dk)
        vs = slice(h * dv, (h + 1) * dv)
        v_h = v_ref[0, :, vs]
        att = jnp.where(mask, _dot_nt(q_dec[:, ks], k_inv[:, ks]), 0.0).astype(bf16)
        o_intra = jnp.dot(att, v_h, preferred_element_type=f32)
        yield
        s = state_ref[h]
        outs = [None] * len(chunks)
        for ci in (reversed(chunks) if reverse else chunks):
            rs = slice(ci * c, (ci + 1) * c)
            outs[ci] = o_intra[rs] + jnp.dot(q_dec[rs, ks], s.astype(bf16), preferred_element_type=f32)
            dec = dec_cols[ci][ks, :]
            s = s * jnp.concatenate([dec] * (dv // LANES), axis=1) + _dot_tn(k_end[rs, ks], v_h[rs])
            yield
        state_ref[h] = s
        o_ref[0, :, vs] = jnp.concatenate(outs, axis=0).astype(o_ref.dtype)


def _gla_kernel(qf_ref, kf_ref, vf_ref, smf_ref, qb_ref, kb_ref, vb_ref, smb_ref,
                w2f_ref, w2b_ref, b2f_ref, b2b_ref, of_ref, ob_ref, state_ref):
    @pl.when(pl.program_id(1) == 0)
    def _():
        state_ref[...] = jnp.zeros_like(state_ref)

    _interleave(_gla_tile(qf_ref, kf_ref, vf_ref, smf_ref, w2f_ref, b2f_ref, of_ref, state_ref.at[0], reverse=False),
                _gla_tile(qb_ref, kb_ref, vb_ref, smb_ref, w2b_ref, b2b_ref, ob_ref, state_ref.at[1], reverse=True))


def _gla(qk, v, small, w2ps, b2s):
    bsz, t, kw2 = qk.shape
    kw = kw2 // 2
    vw = v.shape[-1]
    nt = t // TILE
    const = lambda b, s: (0, 0)
    in_specs, args = [], []
    for reverse in (False, True):
        tile = _scan_tile(nt, reverse)
        tok = lambda cb, tile=tile: (lambda b, s: (b, tile(s), cb))
        in_specs += [pl.BlockSpec((1, TILE, kw), tok(0)),
                     pl.BlockSpec((1, TILE, kw), tok(1)),
                     pl.BlockSpec((1, TILE, vw), tok(0)),
                     pl.BlockSpec((1, TILE, LANES), tok(0))]
        args += [qk, qk, v, small]
    in_specs += [pl.BlockSpec(a.shape, const) for a in (*w2ps, *b2s)]
    args += [*w2ps, *b2s]
    out_specs = [pl.BlockSpec((1, TILE, vw), lambda b, s, tile=_scan_tile(nt, rev): (b, tile(s), 0))
                 for rev in (False, True)]
    return pl.pallas_call(
        _gla_kernel,
        grid=(bsz, nt),
        in_specs=in_specs,
        out_specs=out_specs,
        out_shape=[jax.ShapeDtypeStruct((bsz, t, vw), bf16)] * 2,
        scratch_shapes=[pltpu.VMEM((2, GLA_HEADS, kw // GLA_HEADS, vw // GLA_HEADS), f32)],
        compiler_params=_cparams("parallel", "arbitrary"),
        name="gla",
    )(*args)


def _outmlp_kernel(*refs, n_stream, ff_chunk, final):
    (yf_ref, yb_ref, xs_ref, z_ref, sg_ref, dexp_ref, of_ref, ob_ref, r_ref, gg_ref,
     mod_ref, wo_ref, g2_ref, w1_ref, w2_ref) = refs[n_stream:n_stream + 15]
    o_ref = refs[-1]
    m = mod_ref[0]

    sw = yf_ref.shape[-1]
    gw = sw // SSD_GROUPS
    y = (yf_ref[0] + yb_ref[0]).astype(f32) + dexp_ref[...] * xs_ref[0].astype(f32)
    y = y * _silu(z_ref[0]).astype(f32)
    mix = None
    for g in range(SSD_GROUPS):
        yg = y[:, g * gw:(g + 1) * gw]
        ms = jnp.mean(yg * yg, axis=-1, keepdims=True)
        part = (yg * lax.rsqrt(ms + EPS) * sg_ref[:, g * gw:(g + 1) * gw]).astype(bf16)
        term = jnp.dot(part, wo_ref[g * gw:(g + 1) * gw, :], preferred_element_type=f32)
        mix = term if mix is None else mix + term
    dv = gg_ref.shape[-1]
    o = (of_ref[0] + ob_ref[0]).astype(f32)
    gate = _silu(r_ref[0]).astype(f32)
    for hh in range(GLA_HEADS):
        oh = o[:, hh * dv:(hh + 1) * dv]
        ms = jnp.mean(oh * oh, axis=-1, keepdims=True)
        part = (oh * lax.rsqrt(ms + EPS) * gg_ref[...] * gate[:, hh * dv:(hh + 1) * dv]).astype(bf16)
        mix = mix + jnp.dot(part, wo_ref[sw + hh * dv:sw + (hh + 1) * dv, :], preferred_element_type=f32)
    h1 = _read_stream(refs[:n_stream]) + m[2:3] * mix
    ms = jnp.mean(h1 * h1, axis=-1, keepdims=True)
    u2 = (h1 * lax.rsqrt(ms + EPS) * (g2_ref[...] * (1.0 + m[4:5])) + m[3:4]).astype(bf16)
    acc = jnp.zeros_like(h1)
    for c0 in range(0, w1_ref.shape[-1], ff_chunk):
        hid = jnp.maximum(jnp.dot(u2, w1_ref[:, c0:c0 + ff_chunk], preferred_element_type=f32), 0.0)
        acc = acc + jnp.dot((hid * hid).astype(bf16), w2_ref[c0:c0 + ff_chunk, :], preferred_element_type=f32)
    h2 = h1 + m[5:6] * acc
    if final:
        fg_ref = refs[n_stream + 15]
        ms2 = jnp.mean(h2 * h2, axis=-1, keepdims=True)
        h2 = h2 * lax.rsqrt(ms2 + EPS) * fg_ref[...]
    o_ref[0] = h2


def _outmlp(h, ssd_args, gla_args, mod, wo, g2, w1, w2, ctx_row, final_gain=None):
    yf, yb, xbc, z, sg, dexp = ssd_args
    of, ob, r, gg = gla_args
    bsz, t, sw = yf.shape
    vw = of.shape[-1]
    d = wo.shape[-1]
    nt = t // TILE
    final = final_gain is not None
    skip = 1 if final else 0
    assert not (final and isinstance(h, tuple))
    tok = lambda b, j: (b, j + skip, 0)
    const = lambda b, j: (0, 0)
    resident = lambda a: pl.BlockSpec(a.shape, const, pipeline_mode=pl.Buffered(1))
    if isinstance(h, tuple):
        specs, args = _stream_specs(h, d)
    else:
        specs, args = [pl.BlockSpec((1, TILE, d), tok)], [h]
    n_stream = len(args)
    in_specs = specs + [pl.BlockSpec((1, TILE, sw), tok)] * 4 + [pl.BlockSpec((1, sw), const)] * 2
    in_specs += [pl.BlockSpec((1, TILE, vw), tok)] * 3 + [pl.BlockSpec(gg.shape, const)]
    in_specs += [pl.BlockSpec((1, 6, d), lambda b, j: (jnp.where(j + skip == 0, ctx_row, b), 0, 0)),
                 resident(wo), pl.BlockSpec((1, d), const), resident(w1), resident(w2)]
    args = args + [yf, yb, xbc, z, sg, dexp, of, ob, r, gg, mod, wo, g2, w1, w2]
    if final:
        in_specs.append(pl.BlockSpec((1, d), const))
        args.append(final_gain)
    return pl.pallas_call(
        functools.partial(_outmlp_kernel, n_stream=n_stream, ff_chunk=1024, final=final),
        grid=(bsz, nt - skip),
        in_specs=in_specs,
        out_specs=pl.BlockSpec((1, TILE, d), lambda b, j: (b, j, 0)),
        out_shape=jax.ShapeDtypeStruct((bsz, t - skip * TILE, d), f32),
        compiler_params=_cparams("parallel", "arbitrary"),
        name="outmlp_final" if final else "outmlp",
    )(*args)


def kernel(x, c, ctx, c_ctx, w_ada, b_ada, norm1_g, w_in, conv_w, conv_b, dt_bias, a_log, d_skip,
           ssd_norm_g, gla_w2, gla_b2, gla_norm_g, w_out, norm2_g, w_ff1, w_ff2, final_norm_g):
    bsz, n_lat, d = x.shape
    ctx_len = ctx.shape[1]
    t = ctx_len + n_lat
    depth = w_in.shape[0]
    ssd_w = ssd_norm_g.shape[-1]
    ssd_heads = dt_bias.shape[-1]
    gla_kw = gla_w2.shape[-1]
    gla_dv = gla_norm_g.shape[-1]
    gla_vw = GLA_HEADS * gla_dv
    bc_w = SSD_GROUPS * SSD_STATE
    rank = GLA_GATE_RANK
    assert ctx_len == TILE and n_lat % TILE == 0 and TILE % GRID_W == 0 and n_lat > TILE
    assert ssd_w == ssd_heads * SSD_HEAD_DIM and 2 * ssd_heads + 2 * rank <= LANES
    assert w_in.shape[-1] == 2 * ssd_w + 2 * bc_w + 2 * ssd_heads + 2 * gla_kw + 2 * gla_vw + 2 * rank

    n_rows = -(-(bsz + 1) // SUBLANES) * SUBLANES
    cc = jnp.zeros((n_rows, d), f32).at[:bsz].set(c).at[bsz].set(c_ctx)
    mods = _ada(cc, w_ada, b_ada).reshape(depth, n_rows, 6, d)

    o_dt = 2 * ssd_w + 2 * bc_w
    o_q = o_dt + 2 * ssd_heads
    o_gate = o_q + 2 * gla_kw + 2 * gla_vw
    widths = (ssd_w, ssd_w + 2 * bc_w, 2 * gla_kw, gla_vw, gla_vw, LANES)
    n_small = 2 * ssd_heads + 2 * rank
    w_in16 = w_in.astype(bf16)
    w_cat = jnp.concatenate([w_in16[:, :, :o_dt], w_in16[:, :, o_q:o_gate], w_in16[:, :, o_dt:o_q],
                             w_in16[:, :, o_gate:], jnp.zeros((depth, d, LANES - n_small), bf16)], axis=2)

    def expander(lane_off):
        lanes = jnp.arange(LANES)[:, None]
        heads = jnp.arange(ssd_w)[None, :] // SSD_HEAD_DIM
        e = (lanes == heads + lane_off).astype(bf16)
        return jnp.concatenate([e, e], axis=0)

    def lane_row(vals, off):
        return jnp.zeros((LANES,), f32).at[off:off + vals.shape[0]].set(vals)

    h = (ctx, x)
    for l in range(depth):
        z, xbc, qk, v, r, small = _inproj(h, mods[l], norm1_g[l][None], w_cat[l],
                                          conv_w[l].reshape(CONV_K * CONV_K, -1), conv_b[l][None],
                                          widths, bsz, bsz, t, GRID_W)

        pvecs, ees = [], []
        for dirn in range(2):
            off = dirn * ssd_heads
            pvec = jnp.zeros((SUBLANES, LANES), f32)
            pvec = pvec.at[0].set(lane_row(dt_bias[l, dirn], off)).at[1].set(lane_row(a_log[l, dirn], off))
            pvecs.append(pvec.at[2].set(lane_row(jnp.ones((ssd_heads,), f32), off)))
            ees.append(expander(off))
        y_f, y_b = _ssd(xbc, small, pvecs, ees, ssd_w, ssd_heads)

        w2ps = []
        for dirn in range(2):
            off = 2 * ssd_heads + dirn * rank
            w2ps.append(jnp.zeros((LANES, gla_kw), f32).at[off:off + rank].set(gla_w2[l, dirn]).astype(bf16))
        o_f, o_b = _gla(qk, v, small, w2ps, [gla_b2[l, 0][None], gla_b2[l, 1][None]])

        last = l == depth - 1
        dexp = jnp.repeat(d_skip[l, 0] + d_skip[l, 1], SSD_HEAD_DIM)[None]
        h = _outmlp(h, (y_f, y_b, xbc, z, ssd_norm_g[l][None], dexp), (o_f, o_b, r, gla_norm_g[l][None]),
                    mods[l], w_out[l].astype(bf16), norm2_g[l][None],
                    w_ff1[l].astype(bf16), w_ff2[l].astype(bf16), bsz,
                    final_norm_g[None] if last else None)
    return h
```

```python
import functools

import jax
import jax.numpy as jnp
from jax import lax
from jax.experimental import pallas as pl
from jax.experimental.pallas import tpu as pltpu

f32 = jnp.float32
bf16 = jnp.bfloat16

GRID_W = 64
SSD_HEAD_DIM = 64
SSD_GROUPS = 2
SSD_STATE = 128
SSD_CHUNK = 128
CONV_K = 3
GLA_HEADS = 4
GLA_GATE_RANK = 16
GLA_GATE_NORM = 16.0
GLA_CHUNK = 64
EPS = 1e-6

TILE = 256
LANES = 128
SUBLANES = 8
VMEM_LIMIT = 56 * 1024 * 1024
LOG2E = 1.4426950408889634


def _cparams(*sem):
    return pltpu.CompilerParams(dimension_semantics=sem, vmem_limit_bytes=VMEM_LIMIT)


def _softplus(x):
    return jnp.maximum(x, 0.0) + jnp.log(1.0 + jnp.exp2(-LOG2E * jnp.abs(x)))


def _silu(x):
    if x.dtype == bf16:
        return x / (1.0 + jnp.exp(-x))
    return x / (1.0 + jnp.exp2(-LOG2E * x))


def _split_dot(lhs_bf16, x):
    hi = x.astype(bf16)
    lo = (x - hi.astype(f32)).astype(bf16)
    return (jnp.dot(lhs_bf16, hi, preferred_element_type=f32)
            + jnp.dot(lhs_bf16, lo, preferred_element_type=f32))


def _dot_nt(a, b):
    return lax.dot_general(a, b, (((1,), (1,)), ((), ())), preferred_element_type=f32)


def _dot_tn(a, b):
    return lax.dot_general(a, b, (((0,), (0,)), ((), ())), preferred_element_type=f32)


def _chunk_masks(chunk, reverse):
    ii = lax.broadcasted_iota(jnp.int32, (TILE, TILE), 0)
    jj = lax.broadcasted_iota(jnp.int32, (TILE, TILE), 1)
    same = (ii // chunk) == (jj // chunk)
    return same & ((jj >= ii) if reverse else (jj <= ii))


def _chunk_ends(cum, chunk, reverse):
    n = TILE // chunk
    return [cum[ci * chunk:ci * chunk + 1] if reverse else cum[(ci + 1) * chunk - 1:(ci + 1) * chunk]
            for ci in range(n)]


def _interleave(*streams):
    live = list(streams)
    while live:
        for s in list(live):
            if next(s, StopIteration) is StopIteration:
                live.remove(s)


def _ada_kernel(cc_ref, w_ref, b_ref, o_ref):
    s = _silu(cc_ref[...]).astype(bf16)
    o_ref[0] = jnp.dot(s, w_ref[0].astype(bf16), preferred_element_type=f32) + b_ref[0]


def _ada(cc, w_ada, b_ada):
    depth, d, n = w_ada.shape
    tn = n // 4
    rows = cc.shape[0]
    return pl.pallas_call(
        _ada_kernel,
        grid=(depth, n // tn),
        in_specs=[pl.BlockSpec((rows, d), lambda l, j: (0, 0)),
                  pl.BlockSpec((1, d, tn), lambda l, j: (l, 0, j)),
                  pl.BlockSpec((1, 1, tn), lambda l, j: (l, 0, j))],
        out_specs=pl.BlockSpec((1, rows, tn), lambda l, j: (l, 0, j)),
        out_shape=jax.ShapeDtypeStruct((depth, rows, n), f32),
        compiler_params=_cparams("arbitrary", "arbitrary"),
        name="ada",
    )(cc, w_ada, b_ada.reshape(depth, 1, n))


def _stream_specs(h, d):
    if isinstance(h, tuple):
        ctx, x = h
        return ([pl.BlockSpec((1, TILE, d), lambda b, j: (b, 0, 0)),
                 pl.BlockSpec((1, TILE, d), lambda b, j: (b, jnp.maximum(j - 1, 0), 0))], [ctx, x])
    return [pl.BlockSpec((1, TILE, d), lambda b, j: (b, j, 0))], [h]


def _read_stream(refs):
    if len(refs) == 2:
        return jnp.where(pl.program_id(1) == 0, refs[0][0], refs[1][0])
    return refs[0][0]


def _inproj_kernel(*refs, n_stream, cols, nt, segments):
    up_ref, dn_ref, mod_ref, g_ref, win_ref, cw_ref, cb_ref = refs[n_stream:n_stream + 7]
    z_ref, xbc_ref, qk_ref, v_ref, r_ref, sm_ref, w_ref = refs[n_stream + 7:]
    j = pl.program_id(1)

    @pl.when((pl.program_id(0) == 0) & (j == 0))
    def _():
        dst = 0
        for lo, hi in segments:
            w_ref[:, dst:dst + hi - lo] = win_ref[0, :, lo:hi].astype(bf16)
            dst += hi - lo
        w_ref[:, dst:] = jnp.zeros((w_ref.shape[0], w_ref.shape[1] - dst), bf16)
    m = mod_ref[0]
    gain = g_ref[...] * (1.0 + m[1:2])

    def normed(x):
        ms = jnp.mean(x * x, axis=-1, keepdims=True)
        return (x * lax.rsqrt(ms + EPS) * gain + m[0:1]).astype(bf16)

    u = normed(_read_stream(refs[:n_stream]))
    u_ext = jnp.concatenate([normed(up_ref[0]), u, normed(dn_ref[0])], axis=0)

    zw = z_ref.shape[-1]
    cw = xbc_ref.shape[-1]
    xe = jnp.dot(u_ext, w_ref[:, zw:zw + cw], preferred_element_type=f32)

    is_ctx = j == 0
    lat = jnp.where(is_ctx, 0.0, 1.0)
    top = jnp.where(j <= 1, 0.0, 1.0)
    bot = jnp.where(is_ctx | (j == nt - 1), 0.0, 1.0)
    w = cw_ref[...]
    rows_per_tile = TILE // cols
    parts = []
    for kj in range(CONV_K):
        blocks = []
        for lr in range(rows_per_tile):
            f_up = top if lr == 0 else lat
            f_dn = bot if lr == rows_per_tile - 1 else lat
            blocks.append(xe[lr * cols:(lr + 1) * cols] * (w[kj:kj + 1] * f_up)
                          + xe[(lr + 1) * cols:(lr + 2) * cols] * w[CONV_K + kj:CONV_K + kj + 1]
                          + xe[(lr + 2) * cols:(lr + 3) * cols] * (w[2 * CONV_K + kj:2 * CONV_K + kj + 1] * f_dn))
        parts.append(jnp.concatenate(blocks, axis=0))

    sub = lax.broadcasted_iota(jnp.int32, (SUBLANES, cw), 0)
    rl = pltpu.roll(parts[0], 1, 0)
    rr = pltpu.roll(parts[2], TILE - 1, 0)
    out_blocks = []
    for t0 in range(0, TILE, SUBLANES):
        lb = rl[t0:t0 + SUBLANES]
        rb = rr[t0:t0 + SUBLANES]
        if t0 % cols == 0:
            keep = (sub >= 1) if t0 == 0 else ((sub >= 1) | is_ctx)
            lb = jnp.where(keep, lb, 0.0)
        if (t0 + SUBLANES) % cols == 0:
            keep = (sub <= SUBLANES - 2) if t0 + SUBLANES == TILE else ((sub <= SUBLANES - 2) | is_ctx)
            rb = jnp.where(keep, rb, 0.0)
        out_blocks.append(lb + rb)
    conv = jnp.concatenate(out_blocks, axis=0) + parts[1] + cb_ref[...]
    xbc_ref[0] = _silu(conv).astype(xbc_ref.dtype)

    col = 0
    for ref in (z_ref, xbc_ref, qk_ref, v_ref, r_ref, sm_ref):
        width = ref.shape[-1]
        if ref is not xbc_ref:
            ref[0] = jnp.dot(u, w_ref[:, col:col + width], preferred_element_type=f32).astype(ref.dtype)
        col += width


def _inproj(h, mod, g, w_in, layer, segments, cw9, cbias, widths, ctx_row, bsz, t, cols):
    d = w_in.shape[1]
    nt = t // TILE
    rpt = TILE // cols
    tok = lambda b, j: (b, j, 0)
    specs, args = _stream_specs(h, d)
    if isinstance(h, tuple):
        lat, first, last = h[1], 0, (t - TILE) // cols - 1
    else:
        lat, first, last = h, rpt, t // cols - 1
    up = lambda b, j: (b, jnp.clip(first + (j - 1) * rpt - 1, first, last), 0)
    dn = lambda b, j: (b, jnp.clip(first + j * rpt, first, last), 0)
    const = lambda b, j: (0, 0)
    out_shape = [jax.ShapeDtypeStruct((bsz, t, wd), bf16) for wd in widths[:-1]]
    out_shape.append(jax.ShapeDtypeStruct((bsz, t, widths[-1]), f32))
    return pl.pallas_call(
        functools.partial(_inproj_kernel, n_stream=len(args), cols=cols, nt=nt, segments=segments),
        grid=(bsz, nt),
        in_specs=specs + [pl.BlockSpec((1, cols, d), up), pl.BlockSpec((1, cols, d), dn),
                          pl.BlockSpec((1, 6, d), lambda b, j: (jnp.where(j == 0, ctx_row, b), 0, 0)),
                          pl.BlockSpec((1, d), const),
                          pl.BlockSpec((1,) + w_in.shape[1:], lambda b, j: (layer, 0, 0),
                                       pipeline_mode=pl.Buffered(1)),
                          pl.BlockSpec(cw9.shape, const),
                          pl.BlockSpec(cbias.shape, const)],
        out_specs=[pl.BlockSpec((1, TILE, wd), tok) for wd in widths],
        out_shape=out_shape,
        scratch_shapes=[pltpu.VMEM((d, sum(widths)), bf16)],
        compiler_params=_cparams("arbitrary", "arbitrary"),
        name="inproj",
    )(*args, lat, lat, mod, g, w_in, cw9, cbias)


def _ssd_tile(x_ref, b_ref, c_ref, sm_ref, pv_ref, ee_ref, o_ref, state_ref, *, reverse, lane_off):
    q = SSD_CHUNK
    n = SSD_STATE
    gw = state_ref.shape[-1]
    pairs_per_group = gw // LANES
    hd = SSD_HEAD_DIM

    tri = _chunk_masks(q, reverse).astype(bf16)
    ci_ = lax.broadcasted_iota(jnp.int32, (q, q), 0)
    cj_ = lax.broadcasted_iota(jnp.int32, (q, q), 1)
    mask = (cj_ >= ci_) if reverse else (cj_ <= ci_)
    lo_half = lax.broadcasted_iota(jnp.int32, (q, LANES), 1) < hd
    bias = pv_ref[0:1, :]
    a_coef = -jnp.exp(pv_ref[1:2, :]) * pv_ref[2:3, :] * LOG2E
    ee = ee_ref[...]

    def expand(v):
        hi = v.astype(bf16)
        lo = (v - hi.astype(f32)).astype(bf16)
        return jnp.dot(jnp.concatenate([hi, lo], axis=1), ee, preferred_element_type=f32)

    dt = _softplus(sm_ref[0] + bias)
    ac = _split_dot(tri, dt * a_coef)
    ends = _chunk_ends(ac, q, reverse)
    ac_end = jnp.concatenate([jnp.broadcast_to(e, (q, LANES)) for e in ends], axis=0)
    wexp = expand(dt * jnp.exp2(ac_end - ac)).astype(bf16)
    eexp = expand(jnp.exp2(ac))
    src_t = (ac - jnp.log2(dt)).T
    yield

    chunks = range(TILE // q)
    for ci in (reversed(chunks) if reverse else chunks):
        rs = slice(ci * q, (ci + 1) * q)
        end_row = ci * q if reverse else (ci + 1) * q - 1
        xs = x_ref[0, rs, :]
        xw = xs * wexp[rs]
        for g in range(SSD_GROUPS):
            gs = slice(g * gw, (g + 1) * gw)
            bg = b_ref[0, rs, g * n:(g + 1) * n]
            cg = c_ref[0, rs, g * n:(g + 1) * n]
            cbm = _dot_nt(cg, bg)
            s_g = state_ref[g]
            y_off = jnp.dot(cg, s_g.astype(bf16), preferred_element_type=f32) * eexp[rs, gs]
            ys = []
            for pp in range(pairs_per_group):
                p = g * pairs_per_group + pp
                pieces = []
                for h in (2 * p, 2 * p + 1):
                    li = lane_off + h
                    colb = jnp.broadcast_to(ac[rs, li:li + 1], (q, q))
                    dec = jnp.exp2(jnp.where(mask, colb - src_t[li:li + 1, rs], -jnp.inf))
                    pieces.append((cbm * dec).astype(bf16))
                lhs = jnp.concatenate(pieces, axis=1)
                xp = xs[:, p * LANES:(p + 1) * LANES]
                zero = jnp.zeros_like(xp)
                rhs = jnp.concatenate([jnp.where(lo_half, xp, zero), jnp.where(lo_half, zero, xp)], axis=0)
                ys.append(jnp.dot(lhs, rhs, preferred_element_type=f32))
                yield
            state_ref[g] = s_g * eexp[end_row:end_row + 1, gs] + _dot_tn(bg, xw[:, gs])
            o_ref[0, rs, gs] = (jnp.concatenate(ys, axis=1) + y_off).astype(o_ref.dtype)
            yield


def _ssd_kernel(xf_ref, bf_ref, cf_ref, smf_ref, xb_ref, bb_ref, cb_ref, smb_ref,
                pvf_ref, pvb_ref, eef_ref, eeb_ref, of_ref, ob_ref, state_ref, *, heads):
    @pl.when(pl.program_id(1) == 0)
    def _():
        state_ref[...] = jnp.zeros_like(state_ref)

    _interleave(_ssd_tile(xf_ref, bf_ref, cf_ref, smf_ref, pvf_ref, eef_ref, of_ref, state_ref.at[0],
                          reverse=False, lane_off=0),
                _ssd_tile(xb_ref, bb_ref, cb_ref, smb_ref, pvb_ref, eeb_ref, ob_ref, state_ref.at[1],
                          reverse=True, lane_off=heads))


def _scan_tile(nt, reverse):
    if reverse:
        return lambda s: jnp.where(s == 0, 0, nt - s)
    return lambda s: s


def _ssd(xbc, small, pvecs, ees, width, heads):
    bsz, t, _ = xbc.shape
    nt = t // TILE
    n = SSD_STATE
    gw = width // SSD_GROUPS
    bcb = width // (SSD_GROUPS * n)
    const = lambda b, s: (0, 0)
    in_specs, args = [], []
    for reverse in (False, True):
        tile = _scan_tile(nt, reverse)
        tok = lambda cb, tile=tile: (lambda b, s: (b, tile(s), cb))
        in_specs += [pl.BlockSpec((1, TILE, width), tok(0)),
                     pl.BlockSpec((1, TILE, SSD_GROUPS * n), tok(bcb)),
                     pl.BlockSpec((1, TILE, SSD_GROUPS * n), tok(bcb + 1)),
                     pl.BlockSpec((1, TILE, LANES), tok(0))]
        args += [xbc, xbc, xbc, small]
    in_specs += [pl.BlockSpec(a.shape, const) for a in (*pvecs, *ees)]
    args += [*pvecs, *ees]
    out_specs = [pl.BlockSpec((1, TILE, width), lambda b, s, tile=_scan_tile(nt, rev): (b, tile(s), 0))
                 for rev in (False, True)]
    return pl.pallas_call(
        functools.partial(_ssd_kernel, heads=heads),
        grid=(bsz, nt),
        in_specs=in_specs,
        out_specs=out_specs,
        out_shape=[jax.ShapeDtypeStruct((bsz, t, width), bf16)] * 2,
        scratch_shapes=[pltpu.VMEM((2, SSD_GROUPS, n, gw), f32)],
        compiler_params=_cparams("parallel", "arbitrary"),
        name="ssd",
    )(*args)


def _gla_tile(q_ref, k_ref, v_ref, sm_ref, w2_ref, b2_ref, o_ref, state_ref, *, reverse):
    c = GLA_CHUNK
    dk = state_ref.shape[1]
    dv = state_ref.shape[2]
    kw = GLA_HEADS * dk
    scale = dk ** -0.5

    mask = _chunk_masks(c, reverse)
    tri = mask.astype(bf16)

    gate = jnp.dot(sm_ref[0].astype(bf16), w2_ref[...], preferred_element_type=f32) + b2_ref[...]
    log_a = -_softplus(-gate) * (LOG2E / GLA_GATE_NORM)
    bc = _split_dot(tri, log_a)
    yield
    ends = _chunk_ends(bc, c, reverse)
    b_end = jnp.concatenate([jnp.broadcast_to(e, (c, kw)) for e in ends], axis=0)
    q_dec = q_ref[0] * (jnp.exp2(bc) * scale).astype(bf16)
    k16 = k_ref[0]
    k_inv = k16 * jnp.exp2(-bc).astype(bf16)
    k_end = k16 * jnp.exp2(b_end - bc).astype(bf16)
    dec_cols = [jnp.broadcast_to(jnp.exp2(e), (LANES, kw)).T for e in ends]
    yield

    chunks = range(TILE // c)
    dks = [slice(h * dk, (h + 1) * dk) for h in range(GLA_HEADS)]
    dvs = [slice(h * dv, (h + 1) * dv) for h in range(GLA_HEADS)]
    o_intra = []
    for h in range(GLA_HEADS):
        att = jnp.where(mask, _dot_nt(q_dec[:, dks[h]], k_inv[:, dks[h]]), 0.0).astype(bf16)
        o_intra.append(jnp.dot(att, v_ref[0, :, dvs[h]], preferred_element_type=f32))
        yield
    states = [state_ref[h] for h in range(GLA_HEADS)]
    for ci in (reversed(chunks) if reverse else chunks):
        rs = slice(ci * c, (ci + 1) * c)
        for h in range(GLA_HEADS):
            s = states[h]
            o = o_intra[h][rs] + jnp.dot(q_dec[rs, dks[h]], s.astype(bf16), preferred_element_type=f32)
            o_ref[0, rs, dvs[h]] = o.astype(o_ref.dtype)
            dec = dec_cols[ci][dks[h], :]
            states[h] = (s * jnp.concatenate([dec] * (dv // LANES), axis=1)
                         + _dot_tn(k_end[rs, dks[h]], v_ref[0, rs, dvs[h]]))
            yield
    for h in range(GLA_HEADS):
        state_ref[h] = states[h]


def _gla_kernel(qf_ref, kf_ref, vf_ref, smf_ref, qb_ref, kb_ref, vb_ref, smb_ref,
                w2f_ref, w2b_ref, b2f_ref, b2b_ref, of_ref, ob_ref, state_ref):
    @pl.when(pl.program_id(1) == 0)
    def _():
        state_ref[...] = jnp.zeros_like(state_ref)

    _interleave(_gla_tile(qf_ref, kf_ref, vf_ref, smf_ref, w2f_ref, b2f_ref, of_ref, state_ref.at[0], reverse=False),
                _gla_tile(qb_ref, kb_ref, vb_ref, smb_ref, w2b_ref, b2b_ref, ob_ref, state_ref.at[1], reverse=True))


def _gla(qk, v, small, w2ps, b2s):
    bsz, t, kw2 = qk.shape
    kw = kw2 // 2
    vw = v.shape[-1]
    nt = t // TILE
    const = lambda b, s: (0, 0)
    in_specs, args = [], []
    for reverse in (False, True):
        tile = _scan_tile(nt, reverse)
        tok = lambda cb, tile=tile: (lambda b, s: (b, tile(s), cb))
        in_specs += [pl.BlockSpec((1, TILE, kw), tok(0)),
                     pl.BlockSpec((1, TILE, kw), tok(1)),
                     pl.BlockSpec((1, TILE, vw), tok(0)),
                     pl.BlockSpec((1, TILE, LANES), tok(0))]
        args += [qk, qk, v, small]
    in_specs += [pl.BlockSpec(a.shape, const) for a in (*w2ps, *b2s)]
    args += [*w2ps, *b2s]
    out_specs = [pl.BlockSpec((1, TILE, vw), lambda b, s, tile=_scan_tile(nt, rev): (b, tile(s), 0))
                 for rev in (False, True)]
    return pl.pallas_call(
        _gla_kernel,
        grid=(bsz, nt),
        in_specs=in_specs,
        out_specs=out_specs,
        out_shape=[jax.ShapeDtypeStruct((bsz, t, vw), bf16)] * 2,
        scratch_shapes=[pltpu.VMEM((2, GLA_HEADS, kw // GLA_HEADS, vw // GLA_HEADS), f32)],
        compiler_params=_cparams("parallel", "arbitrary"),
        name="gla",
    )(*args)


def _outmlp_kernel(*refs, n_stream, ff_chunk, final):
    (yf_ref, yb_ref, xs_ref, z_ref, sg_ref, dexp_ref, of_ref, ob_ref, r_ref, gg_ref,
     mod_ref, wo_ref, g2_ref, w1_ref, w2_ref) = refs[n_stream:n_stream + 15]
    o_ref = refs[-1]
    m = mod_ref[0]

    sw = yf_ref.shape[-1]
    gw = sw // SSD_GROUPS
    y = (yf_ref[0] + yb_ref[0]).astype(f32) + dexp_ref[...] * xs_ref[0].astype(f32)
    y = y * _silu(z_ref[0]).astype(f32)
    mix = None
    for g in range(SSD_GROUPS):
        yg = y[:, g * gw:(g + 1) * gw]
        ms = jnp.mean(yg * yg, axis=-1, keepdims=True)
        part = (yg * lax.rsqrt(ms + EPS) * sg_ref[:, g * gw:(g + 1) * gw]).astype(bf16)
        term = jnp.dot(part, wo_ref[g * gw:(g + 1) * gw, :], preferred_element_type=f32)
        mix = term if mix is None else mix + term
    dv = gg_ref.shape[-1]
    o = (of_ref[0] + ob_ref[0]).astype(f32)
    gate = _silu(r_ref[0]).astype(f32)
    for hh in range(GLA_HEADS):
        oh = o[:, hh * dv:(hh + 1) * dv]
        ms = jnp.mean(oh * oh, axis=-1, keepdims=True)
        part = (oh * lax.rsqrt(ms + EPS) * gg_ref[...] * gate[:, hh * dv:(hh + 1) * dv]).astype(bf16)
        mix = mix + jnp.dot(part, wo_ref[sw + hh * dv:sw + (hh + 1) * dv, :], preferred_element_type=f32)
    h1 = _read_stream(refs[:n_stream]) + m[2:3] * mix
    ms = jnp.mean(h1 * h1, axis=-1, keepdims=True)
    u2 = (h1 * lax.rsqrt(ms + EPS) * (g2_ref[...] * (1.0 + m[4:5])) + m[3:4]).astype(bf16)
    acc = jnp.zeros_like(h1)
    for c0 in range(0, w1_ref.shape[-1], ff_chunk):
        hid = jnp.maximum(jnp.dot(u2, w1_ref[:, c0:c0 + ff_chunk], preferred_element_type=f32), 0.0)
        acc = acc + jnp.dot((hid * hid).astype(bf16), w2_ref[c0:c0 + ff_chunk, :], preferred_element_type=f32)
    h2 = h1 + m[5:6] * acc
    if final:
        fg_ref = refs[n_stream + 15]
        ms2 = jnp.mean(h2 * h2, axis=-1, keepdims=True)
        h2 = h2 * lax.rsqrt(ms2 + EPS) * fg_ref[...]
    o_ref[0] = h2


def _outmlp(h, ssd_args, gla_args, mod, wo, g2, w1, w2, ctx_row, final_gain=None):
    yf, yb, xbc, z, sg, dexp = ssd_args
    of, ob, r, gg = gla_args
    bsz, t, sw = yf.shape
    vw = of.shape[-1]
    d = wo.shape[-1]
    nt = t // TILE
    final = final_gain is not None
    skip = 1 if final else 0
    assert not (final and isinstance(h, tuple))
    tok = lambda b, j: (b, j + skip, 0)
    const = lambda b, j: (0, 0)
    resident = lambda a: pl.BlockSpec(a.shape, const, pipeline_mode=pl.Buffered(1))
    if isinstance(h, tuple):
        specs, args = _stream_specs(h, d)
    else:
        specs, args = [pl.BlockSpec((1, TILE, d), tok)], [h]
    n_stream = len(args)
    in_specs = specs + [pl.BlockSpec((1, TILE, sw), tok)] * 4 + [pl.BlockSpec((1, sw), const)] * 2
    in_specs += [pl.BlockSpec((1, TILE, vw), tok)] * 3 + [pl.BlockSpec(gg.shape, const)]
    in_specs += [pl.BlockSpec((1, 6, d), lambda b, j: (jnp.where(j + skip == 0, ctx_row, b), 0, 0)),
                 resident(wo), pl.BlockSpec((1, d), const), resident(w1), resident(w2)]
    args = args + [yf, yb, xbc, z, sg, dexp, of, ob, r, gg, mod, wo, g2, w1, w2]
    if final:
        in_specs.append(pl.BlockSpec((1, d), const))
        args.append(final_gain)
    return pl.pallas_call(
        functools.partial(_outmlp_kernel, n_stream=n_stream, ff_chunk=1024, final=final),
        grid=(bsz, nt - skip),
        in_specs=in_specs,
        out_specs=pl.BlockSpec((1, TILE, d), lambda b, j: (b, j, 0)),
        out_shape=jax.ShapeDtypeStruct((bsz, t - skip * TILE, d), f32),
        compiler_params=_cparams("parallel", "arbitrary"),
        name="outmlp_final" if final else "outmlp",
    )(*args)


def kernel(x, c, ctx, c_ctx, w_ada, b_ada, norm1_g, w_in, conv_w, conv_b, dt_bias, a_log, d_skip,
           ssd_norm_g, gla_w2, gla_b2, gla_norm_g, w_out, norm2_g, w_ff1, w_ff2, final_norm_g):
    bsz, n_lat, d = x.shape
    ctx_len = ctx.shape[1]
    t = ctx_len + n_lat
    depth = w_in.shape[0]
    ssd_w = ssd_norm_g.shape[-1]
    ssd_heads = dt_bias.shape[-1]
    gla_kw = gla_w2.shape[-1]
    gla_dv = gla_norm_g.shape[-1]
    gla_vw = GLA_HEADS * gla_dv
    bc_w = SSD_GROUPS * SSD_STATE
    rank = GLA_GATE_RANK
    assert ctx_len == TILE and n_lat % TILE == 0 and TILE % GRID_W == 0 and n_lat > TILE
    assert ssd_w == ssd_heads * SSD_HEAD_DIM and 2 * ssd_heads + 2 * rank <= LANES
    assert w_in.shape[-1] == 2 * ssd_w + 2 * bc_w + 2 * ssd_heads + 2 * gla_kw + 2 * gla_vw + 2 * rank

    n_rows = -(-(bsz + 1) // SUBLANES) * SUBLANES
    cc = jnp.zeros((n_rows, d), f32).at[:bsz].set(c).at[bsz].set(c_ctx)
    mods = _ada(cc, w_ada, b_ada).reshape(depth, n_rows, 6, d)

    o_dt = 2 * ssd_w + 2 * bc_w
    o_q = o_dt + 2 * ssd_heads
    o_gate = o_q + 2 * gla_kw + 2 * gla_vw
    widths = (ssd_w, ssd_w + 2 * bc_w, 2 * gla_kw, gla_vw, gla_vw, LANES)
    segments = ((0, o_dt), (o_q, o_gate), (o_dt, o_q), (o_gate, o_gate + 2 * rank))

    def expander(lane_off):
        lanes = jnp.arange(LANES)[:, None]
        heads = jnp.arange(ssd_w)[None, :] // SSD_HEAD_DIM
        e = (lanes == heads + lane_off).astype(bf16)
        return jnp.concatenate([e, e], axis=0)

    def lane_row(vals, off):
        return jnp.zeros((LANES,), f32).at[off:off + vals.shape[0]].set(vals)

    h = (ctx, x)
    for l in range(depth):
        z, xbc, qk, v, r, small = _inproj(h, mods[l], norm1_g[l][None], w_in, l, segments,
                                          conv_w[l].reshape(CONV_K * CONV_K, -1), conv_b[l][None],
                                          widths, bsz, bsz, t, GRID_W)

        pvecs, ees = [], []
        for dirn in range(2):
            off = dirn * ssd_heads
            pvec = jnp.zeros((SUBLANES, LANES), f32)
            pvec = pvec.at[0].set(lane_row(dt_bias[l, dirn], off)).at[1].set(lane_row(a_log[l, dirn], off))
            pvecs.append(pvec.at[2].set(lane_row(jnp.ones((ssd_heads,), f32), off)))
            ees.append(expander(off))
        y_f, y_b = _ssd(xbc, small, pvecs, ees, ssd_w, ssd_heads)

        w2ps = []
        for dirn in range(2):
            off = 2 * ssd_heads + dirn * rank
            w2ps.append(jnp.zeros((LANES, gla_kw), f32).at[off:off + rank].set(gla_w2[l, dirn]).astype(bf16))
        o_f, o_b = _gla(qk, v, small, w2ps, [gla_b2[l, 0][None], gla_b2[l, 1][None]])

        last = l == depth - 1
        dexp = jnp.repeat(d_skip[l, 0] + d_skip[l, 1], SSD_HEAD_DIM)[None]
        h = _outmlp(h, (y_f, y_b, xbc, z, ssd_norm_g[l][None], dexp), (o_f, o_b, r, gla_norm_g[l][None]),
                    mods[l], w_out[l].astype(bf16), norm2_g[l][None],
                    w_ff1[l].astype(bf16), w_ff2[l].astype(bf16), bsz,
                    final_norm_g[None] if last else None)
    return h
```

```python
import functools

import jax
import jax.numpy as jnp
from jax import lax
from jax.experimental import pallas as pl
from jax.experimental.pallas import tpu as pltpu

f32 = jnp.float32
bf16 = jnp.bfloat16

GRID_W = 64
SSD_HEAD_DIM = 64
SSD_GROUPS = 2
SSD_STATE = 128
SSD_CHUNK = 128
CONV_K = 3
GLA_HEADS = 4
GLA_GATE_RANK = 16
GLA_GATE_NORM = 16.0
GLA_CHUNK = 64
EPS = 1e-6

TILE = 256
LANES = 128
SUBLANES = 8
VMEM_LIMIT = 56 * 1024 * 1024
LOG2E = 1.4426950408889634


def _cparams(*sem):
    return pltpu.CompilerParams(dimension_semantics=sem, vmem_limit_bytes=VMEM_LIMIT)


def _softplus(x):
    return jnp.maximum(x, 0.0) + jnp.log(1.0 + jnp.exp2(-LOG2E * jnp.abs(x)))


def _silu(x):
    if x.dtype == bf16:
        return x / (1.0 + jnp.exp(-x))
    return x / (1.0 + jnp.exp2(-LOG2E * x))


def _split_dot(lhs_bf16, x):
    hi = x.astype(bf16)
    lo = (x - hi.astype(f32)).astype(bf16)
    return (jnp.dot(lhs_bf16, hi, preferred_element_type=f32)
            + jnp.dot(lhs_bf16, lo, preferred_element_type=f32))


def _dot_nt(a, b):
    return lax.dot_general(a, b, (((1,), (1,)), ((), ())), preferred_element_type=f32)


def _dot_tn(a, b):
    return lax.dot_general(a, b, (((0,), (0,)), ((), ())), preferred_element_type=f32)


def _chunk_masks(chunk, reverse):
    ii = lax.broadcasted_iota(jnp.int32, (TILE, TILE), 0)
    jj = lax.broadcasted_iota(jnp.int32, (TILE, TILE), 1)
    same = (ii // chunk) == (jj // chunk)
    return same & ((jj >= ii) if reverse else (jj <= ii))


def _chunk_ends(cum, chunk, reverse):
    n = TILE // chunk
    return [cum[ci * chunk:ci * chunk + 1] if reverse else cum[(ci + 1) * chunk - 1:(ci + 1) * chunk]
            for ci in range(n)]


def _interleave(*streams):
    live = list(streams)
    while live:
        for s in list(live):
            if next(s, StopIteration) is StopIteration:
                live.remove(s)


def _ada_kernel(cc_ref, w_ref, b_ref, o_ref):
    s = _silu(cc_ref[...]).astype(bf16)
    o_ref[0] = jnp.dot(s, w_ref[0].astype(bf16), preferred_element_type=f32) + b_ref[0]


def _ada(cc, w_ada, b_ada):
    depth, d, n = w_ada.shape
    tn = n // 4
    rows = cc.shape[0]
    return pl.pallas_call(
        _ada_kernel,
        grid=(depth, n // tn),
        in_specs=[pl.BlockSpec((rows, d), lambda l, j: (0, 0)),
                  pl.BlockSpec((1, d, tn), lambda l, j: (l, 0, j)),
                  pl.BlockSpec((1, 1, tn), lambda l, j: (l, 0, j))],
        out_specs=pl.BlockSpec((1, rows, tn), lambda l, j: (l, 0, j)),
        out_shape=jax.ShapeDtypeStruct((depth, rows, n), f32),
        compiler_params=_cparams("arbitrary", "arbitrary"),
        name="ada",
    )(cc, w_ada, b_ada.reshape(depth, 1, n))


def _stream_specs(h, d):
    if isinstance(h, tuple):
        ctx, x = h
        return ([pl.BlockSpec((1, TILE, d), lambda b, j: (b, 0, 0)),
                 pl.BlockSpec((1, TILE, d), lambda b, j: (b, jnp.maximum(j - 1, 0), 0))], [ctx, x])
    return [pl.BlockSpec((1, TILE, d), lambda b, j: (b, j, 0))], [h]


def _read_stream(refs):
    if len(refs) == 2:
        return jnp.where(pl.program_id(1) == 0, refs[0][0], refs[1][0])
    return refs[0][0]


def _inproj_kernel(*refs, n_stream, cols, nt, segments):
    up_ref, dn_ref, mod_ref, g_ref, win_ref, cw_ref, cb_ref = refs[n_stream:n_stream + 7]
    z_ref, xbc_ref, qk_ref, v_ref, r_ref, sm_ref, w_ref = refs[n_stream + 7:]
    j = pl.program_id(1)

    @pl.when((pl.program_id(0) == 0) & (j == 0))
    def _():
        dst = 0
        for lo, hi in segments:
            w_ref[dst:dst + hi - lo, :] = win_ref[0, lo:hi, :].astype(bf16)
            dst += hi - lo
        w_ref[dst:, :] = jnp.zeros((w_ref.shape[0] - dst, w_ref.shape[1]), bf16)
    m = mod_ref[0]
    gain = g_ref[...] * (1.0 + m[1:2])

    def normed(x):
        ms = jnp.mean(x * x, axis=-1, keepdims=True)
        return (x * lax.rsqrt(ms + EPS) * gain + m[0:1]).astype(bf16)

    u = normed(_read_stream(refs[:n_stream]))
    u_ext = jnp.concatenate([normed(up_ref[0]), u, normed(dn_ref[0])], axis=0)

    zw = z_ref.shape[-1]
    cw = xbc_ref.shape[-1]
    xe = _dot_nt(u_ext, w_ref[zw:zw + cw, :])

    is_ctx = j == 0
    lat = jnp.where(is_ctx, 0.0, 1.0)
    top = jnp.where(j <= 1, 0.0, 1.0)
    bot = jnp.where(is_ctx | (j == nt - 1), 0.0, 1.0)
    w = cw_ref[...]
    rows_per_tile = TILE // cols
    parts = []
    for kj in range(CONV_K):
        blocks = []
        for lr in range(rows_per_tile):
            f_up = top if lr == 0 else lat
            f_dn = bot if lr == rows_per_tile - 1 else lat
            blocks.append(xe[lr * cols:(lr + 1) * cols] * (w[kj:kj + 1] * f_up)
                          + xe[(lr + 1) * cols:(lr + 2) * cols] * w[CONV_K + kj:CONV_K + kj + 1]
                          + xe[(lr + 2) * cols:(lr + 3) * cols] * (w[2 * CONV_K + kj:2 * CONV_K + kj + 1] * f_dn))
        parts.append(jnp.concatenate(blocks, axis=0))

    sub = lax.broadcasted_iota(jnp.int32, (SUBLANES, cw), 0)
    rl = pltpu.roll(parts[0], 1, 0)
    rr = pltpu.roll(parts[2], TILE - 1, 0)
    out_blocks = []
    for t0 in range(0, TILE, SUBLANES):
        lb = rl[t0:t0 + SUBLANES]
        rb = rr[t0:t0 + SUBLANES]
        if t0 % cols == 0:
            keep = (sub >= 1) if t0 == 0 else ((sub >= 1) | is_ctx)
            lb = jnp.where(keep, lb, 0.0)
        if (t0 + SUBLANES) % cols == 0:
            keep = (sub <= SUBLANES - 2) if t0 + SUBLANES == TILE else ((sub <= SUBLANES - 2) | is_ctx)
            rb = jnp.where(keep, rb, 0.0)
        out_blocks.append(lb + rb)
    conv = jnp.concatenate(out_blocks, axis=0) + parts[1] + cb_ref[...]
    xbc_ref[0] = _silu(conv).astype(xbc_ref.dtype)

    col = 0
    for ref in (z_ref, xbc_ref, qk_ref, v_ref, r_ref, sm_ref):
        width = ref.shape[-1]
        if ref is not xbc_ref:
            ref[0] = _dot_nt(u, w_ref[col:col + width, :]).astype(ref.dtype)
        col += width


def _inproj(h, mod, g, w_in, layer, segments, cw9, cbias, widths, ctx_row, bsz, t, cols):
    d = w_in.shape[2]
    nt = t // TILE
    rpt = TILE // cols
    tok = lambda b, j: (b, j, 0)
    specs, args = _stream_specs(h, d)
    if isinstance(h, tuple):
        lat, first, last = h[1], 0, (t - TILE) // cols - 1
    else:
        lat, first, last = h, rpt, t // cols - 1
    up = lambda b, j: (b, jnp.clip(first + (j - 1) * rpt - 1, first, last), 0)
    dn = lambda b, j: (b, jnp.clip(first + j * rpt, first, last), 0)
    const = lambda b, j: (0, 0)
    out_shape = [jax.ShapeDtypeStruct((bsz, t, wd), bf16) for wd in widths[:-1]]
    out_shape.append(jax.ShapeDtypeStruct((bsz, t, widths[-1]), f32))
    return pl.pallas_call(
        functools.partial(_inproj_kernel, n_stream=len(args), cols=cols, nt=nt, segments=segments),
        grid=(bsz, nt),
        in_specs=specs + [pl.BlockSpec((1, cols, d), up), pl.BlockSpec((1, cols, d), dn),
                          pl.BlockSpec((1, 6, d), lambda b, j: (jnp.where(j == 0, ctx_row, b), 0, 0)),
                          pl.BlockSpec((1, d), const),
                          pl.BlockSpec((1,) + w_in.shape[1:], lambda b, j: (layer, 0, 0),
                                       pipeline_mode=pl.Buffered(1)),
                          pl.BlockSpec(cw9.shape, const),
                          pl.BlockSpec(cbias.shape, const)],
        out_specs=[pl.BlockSpec((1, TILE, wd), tok) for wd in widths],
        out_shape=out_shape,
        scratch_shapes=[pltpu.VMEM((sum(widths), d), bf16)],
        compiler_params=_cparams("arbitrary", "arbitrary"),
        name="inproj",
    )(*args, lat, lat, mod, g, w_in, cw9, cbias)


def _ssd_tile(x_ref, b_ref, c_ref, sm_ref, pv_ref, ee_ref, o_ref, state_ref, *, reverse, lane_off):
    q = SSD_CHUNK
    n = SSD_STATE
    gw = state_ref.shape[-1]
    pairs_per_group = gw // LANES
    hd = SSD_HEAD_DIM

    tri = _chunk_masks(q, reverse).astype(bf16)
    ci_ = lax.broadcasted_iota(jnp.int32, (q, q), 0)
    cj_ = lax.broadcasted_iota(jnp.int32, (q, q), 1)
    mask = (cj_ >= ci_) if reverse else (cj_ <= ci_)
    lo_half = lax.broadcasted_iota(jnp.int32, (q, LANES), 1) < hd
    bias = pv_ref[0:1, :]
    a_coef = -jnp.exp(pv_ref[1:2, :]) * pv_ref[2:3, :] * LOG2E
    ee = ee_ref[...]

    def expand(v):
        hi = v.astype(bf16)
        lo = (v - hi.astype(f32)).astype(bf16)
        return jnp.dot(jnp.concatenate([hi, lo], axis=1), ee, preferred_element_type=f32)

    dt = _softplus(sm_ref[0] + bias)
    ac = _split_dot(tri, dt * a_coef)
    ends = _chunk_ends(ac, q, reverse)
    ac_end = jnp.concatenate([jnp.broadcast_to(e, (q, LANES)) for e in ends], axis=0)
    wexp = expand(dt * jnp.exp2(ac_end - ac)).astype(bf16)
    eexp = expand(jnp.exp2(ac))
    src_t = (ac - jnp.log2(dt)).T
    yield

    chunks = range(TILE // q)
    for ci in (reversed(chunks) if reverse else chunks):
        rs = slice(ci * q, (ci + 1) * q)
        end_row = ci * q if reverse else (ci + 1) * q - 1
        xs = x_ref[0, rs, :]
        xw = xs * wexp[rs]
        for g in range(SSD_GROUPS):
            gs = slice(g * gw, (g + 1) * gw)
            bg = b_ref[0, rs, g * n:(g + 1) * n]
            cg = c_ref[0, rs, g * n:(g + 1) * n]
            cbm = _dot_nt(cg, bg)
            s_g = state_ref[g]
            y_off = jnp.dot(cg, s_g.astype(bf16), preferred_element_type=f32) * eexp[rs, gs]
            ys = []
            for pp in range(pairs_per_group):
                p = g * pairs_per_group + pp
                pieces = []
                for h in (2 * p, 2 * p + 1):
                    li = lane_off + h
                    colb = jnp.broadcast_to(ac[rs, li:li + 1], (q, q))
                    dec = jnp.exp2(jnp.where(mask, colb - src_t[li:li + 1, rs], -jnp.inf))
                    pieces.append((cbm * dec).astype(bf16))
                lhs = jnp.concatenate(pieces, axis=1)
                xp = xs[:, p * LANES:(p + 1) * LANES]
                zero = jnp.zeros_like(xp)
                rhs = jnp.concatenate([jnp.where(lo_half, xp, zero), jnp.where(lo_half, zero, xp)], axis=0)
                ys.append(jnp.dot(lhs, rhs, preferred_element_type=f32))
                yield
            state_ref[g] = s_g * eexp[end_row:end_row + 1, gs] + _dot_tn(bg, xw[:, gs])
            o_ref[0, rs, gs] = (jnp.concatenate(ys, axis=1) + y_off).astype(o_ref.dtype)
            yield


def _ssd_kernel(xf_ref, bf_ref, cf_ref, smf_ref, xb_ref, bb_ref, cb_ref, smb_ref,
                pvf_ref, pvb_ref, eef_ref, eeb_ref, of_ref, ob_ref, state_ref, *, heads):
    @pl.when(pl.program_id(1) == 0)
    def _():
        state_ref[...] = jnp.zeros_like(state_ref)

    _interleave(_ssd_tile(xf_ref, bf_ref, cf_ref, smf_ref, pvf_ref, eef_ref, of_ref, state_ref.at[0],
                          reverse=False, lane_off=0),
                _ssd_tile(xb_ref, bb_ref, cb_ref, smb_ref, pvb_ref, eeb_ref, ob_ref, state_ref.at[1],
                          reverse=True, lane_off=heads))


def _scan_tile(nt, reverse):
    if reverse:
        return lambda s: jnp.where(s == 0, 0, nt - s)
    return lambda s: s


def _ssd(xbc, small, pvecs, ees, width, heads):
    bsz, t, _ = xbc.shape
    nt = t // TILE
    n = SSD_STATE
    gw = width // SSD_GROUPS
    bcb = width // (SSD_GROUPS * n)
    const = lambda b, s: (0, 0)
    in_specs, args = [], []
    for reverse in (False, True):
        tile = _scan_tile(nt, reverse)
        tok = lambda cb, tile=tile: (lambda b, s: (b, tile(s), cb))
        in_specs += [pl.BlockSpec((1, TILE, width), tok(0)),
                     pl.BlockSpec((1, TILE, SSD_GROUPS * n), tok(bcb)),
                     pl.BlockSpec((1, TILE, SSD_GROUPS * n), tok(bcb + 1)),
                     pl.BlockSpec((1, TILE, LANES), tok(0))]
        args += [xbc, xbc, xbc, small]
    in_specs += [pl.BlockSpec(a.shape, const) for a in (*pvecs, *ees)]
    args += [*pvecs, *ees]
    out_specs = [pl.BlockSpec((1, TILE, width), lambda b, s, tile=_scan_tile(nt, rev): (b, tile(s), 0))
                 for rev in (False, True)]
    return pl.pallas_call(
        functools.partial(_ssd_kernel, heads=heads),
        grid=(bsz, nt),
        in_specs=in_specs,
        out_specs=out_specs,
        out_shape=[jax.ShapeDtypeStruct((bsz, t, width), bf16)] * 2,
        scratch_shapes=[pltpu.VMEM((2, SSD_GROUPS, n, gw), f32)],
        compiler_params=_cparams("parallel", "arbitrary"),
        name="ssd",
    )(*args)


def _gla_tile(q_ref, k_ref, v_ref, sm_ref, w2_ref, b2_ref, o_ref, state_ref, *, reverse):
    c = GLA_CHUNK
    dk = state_ref.shape[1]
    dv = state_ref.shape[2]
    kw = GLA_HEADS * dk
    scale = dk ** -0.5

    mask = _chunk_masks(c, reverse)
    tri = mask.astype(bf16)

    gate = jnp.dot(sm_ref[0].astype(bf16), w2_ref[...], preferred_element_type=f32) + b2_ref[...]
    log_a = -_softplus(-gate) * (LOG2E / GLA_GATE_NORM)
    bc = _split_dot(tri, log_a)
    yield
    ends = _chunk_ends(bc, c, reverse)
    b_end = jnp.concatenate([jnp.broadcast_to(e, (c, kw)) for e in ends], axis=0)
    q_dec = q_ref[0] * (jnp.exp2(bc) * scale).astype(bf16)
    k16 = k_ref[0]
    k_inv = k16 * jnp.exp2(-bc).astype(bf16)
    k_end = k16 * jnp.exp2(b_end - bc).astype(bf16)
    dec_cols = [jnp.broadcast_to(jnp.exp2(e), (LANES, kw)).T for e in ends]
    yield

    chunks = range(TILE // c)
    dks = [slice(h * dk, (h + 1) * dk) for h in range(GLA_HEADS)]
    dvs = [slice(h * dv, (h + 1) * dv) for h in range(GLA_HEADS)]
    o_intra = []
    for h in range(GLA_HEADS):
        att = jnp.where(mask, _dot_nt(q_dec[:, dks[h]], k_inv[:, dks[h]]), 0.0).astype(bf16)
        o_intra.append(jnp.dot(att, v_ref[0, :, dvs[h]], preferred_element_type=f32))
        yield
    states = [state_ref[h] for h in range(GLA_HEADS)]
    for ci in (reversed(chunks) if reverse else chunks):
        rs = slice(ci * c, (ci + 1) * c)
        for h in range(GLA_HEADS):
            s = states[h]
            o = o_intra[h][rs] + jnp.dot(q_dec[rs, dks[h]], s.astype(bf16), preferred_element_type=f32)
            o_ref[0, rs, dvs[h]] = o.astype(o_ref.dtype)
            dec = dec_cols[ci][dks[h], :]
            states[h] = (s * jnp.concatenate([dec] * (dv // LANES), axis=1)
                         + _dot_tn(k_end[rs, dks[h]], v_ref[0, rs, dvs[h]]))
            yield
    for h in range(GLA_HEADS):
        state_ref[h] = states[h]


def _gla_kernel(qf_ref, kf_ref, vf_ref, smf_ref, qb_ref, kb_ref, vb_ref, smb_ref,
                w2f_ref, w2b_ref, b2f_ref, b2b_ref, of_ref, ob_ref, state_ref):
    @pl.when(pl.program_id(1) == 0)
    def _():
        state_ref[...] = jnp.zeros_like(state_ref)

    _interleave(_gla_tile(qf_ref, kf_ref, vf_ref, smf_ref, w2f_ref, b2f_ref, of_ref, state_ref.at[0], reverse=False),
                _gla_tile(qb_ref, kb_ref, vb_ref, smb_ref, w2b_ref, b2b_ref, ob_ref, state_ref.at[1], reverse=True))


def _gla(qk, v, small, w2ps, b2s):
    bsz, t, kw2 = qk.shape
    kw = kw2 // 2
    vw = v.shape[-1]
    nt = t // TILE
    const = lambda b, s: (0, 0)
    in_specs, args = [], []
    for reverse in (False, True):
        tile = _scan_tile(nt, reverse)
        tok = lambda cb, tile=tile: (lambda b, s: (b, tile(s), cb))
        in_specs += [pl.BlockSpec((1, TILE, kw), tok(0)),
                     pl.BlockSpec((1, TILE, kw), tok(1)),
                     pl.BlockSpec((1, TILE, vw), tok(0)),
                     pl.BlockSpec((1, TILE, LANES), tok(0))]
        args += [qk, qk, v, small]
    in_specs += [pl.BlockSpec(a.shape, const) for a in (*w2ps, *b2s)]
    args += [*w2ps, *b2s]
    out_specs = [pl.BlockSpec((1, TILE, vw), lambda b, s, tile=_scan_tile(nt, rev): (b, tile(s), 0))
                 for rev in (False, True)]
    return pl.pallas_call(
        _gla_kernel,
        grid=(bsz, nt),
        in_specs=in_specs,
        out_specs=out_specs,
        out_shape=[jax.ShapeDtypeStruct((bsz, t, vw), bf16)] * 2,
        scratch_shapes=[pltpu.VMEM((2, GLA_HEADS, kw // GLA_HEADS, vw // GLA_HEADS), f32)],
        compiler_params=_cparams("parallel", "arbitrary"),
        name="gla",
    )(*args)


def _outmlp_kernel(*refs, n_stream, ff_chunk, final):
    (yf_ref, yb_ref, xs_ref, z_ref, sg_ref, dexp_ref, of_ref, ob_ref, r_ref, gg_ref,
     mod_ref, wo_ref, g2_ref, w1_ref, w2_ref) = refs[n_stream:n_stream + 15]
    o_ref = refs[-1]
    m = mod_ref[0]

    sw = yf_ref.shape[-1]
    gw = sw // SSD_GROUPS
    y = (yf_ref[0] + yb_ref[0]).astype(f32) + dexp_ref[...] * xs_ref[0].astype(f32)
    y = y * _silu(z_ref[0]).astype(f32)
    mix = None
    for g in range(SSD_GROUPS):
        yg = y[:, g * gw:(g + 1) * gw]
        ms = jnp.mean(yg * yg, axis=-1, keepdims=True)
        part = (yg * lax.rsqrt(ms + EPS) * sg_ref[:, g * gw:(g + 1) * gw]).astype(bf16)
        term = jnp.dot(part, wo_ref[g * gw:(g + 1) * gw, :], preferred_element_type=f32)
        mix = term if mix is None else mix + term
    dv = gg_ref.shape[-1]
    o = (of_ref[0] + ob_ref[0]).astype(f32)
    gate = _silu(r_ref[0]).astype(f32)
    for hh in range(GLA_HEADS):
        oh = o[:, hh * dv:(hh + 1) * dv]
        ms = jnp.mean(oh * oh, axis=-1, keepdims=True)
        part = (oh * lax.rsqrt(ms + EPS) * gg_ref[...] * gate[:, hh * dv:(hh + 1) * dv]).astype(bf16)
        mix = mix + jnp.dot(part, wo_ref[sw + hh * dv:sw + (hh + 1) * dv, :], preferred_element_type=f32)
    h1 = _read_stream(refs[:n_stream]) + m[2:3] * mix
    ms = jnp.mean(h1 * h1, axis=-1, keepdims=True)
    u2 = (h1 * lax.rsqrt(ms + EPS) * (g2_ref[...] * (1.0 + m[4:5])) + m[3:4]).astype(bf16)
    acc = jnp.zeros_like(h1)
    for c0 in range(0, w1_ref.shape[-1], ff_chunk):
        hid = jnp.maximum(jnp.dot(u2, w1_ref[:, c0:c0 + ff_chunk], preferred_element_type=f32), 0.0)
        acc = acc + jnp.dot((hid * hid).astype(bf16), w2_ref[c0:c0 + ff_chunk, :], preferred_element_type=f32)
    h2 = h1 + m[5:6] * acc
    if final:
        fg_ref = refs[n_stream + 15]
        ms2 = jnp.mean(h2 * h2, axis=-1, keepdims=True)
        h2 = h2 * lax.rsqrt(ms2 + EPS) * fg_ref[...]
    o_ref[0] = h2


def _outmlp(h, ssd_args, gla_args, mod, wo, g2, w1, w2, ctx_row, final_gain=None):
    yf, yb, xbc, z, sg, dexp = ssd_args
    of, ob, r, gg = gla_args
    bsz, t, sw = yf.shape
    vw = of.shape[-1]
    d = wo.shape[-1]
    nt = t // TILE
    final = final_gain is not None
    skip = 1 if final else 0
    assert not (final and isinstance(h, tuple))
    tok = lambda b, j: (b, j + skip, 0)
    const = lambda b, j: (0, 0)
    resident = lambda a: pl.BlockSpec(a.shape, const, pipeline_mode=pl.Buffered(1))
    if isinstance(h, tuple):
        specs, args = _stream_specs(h, d)
    else:
        specs, args = [pl.BlockSpec((1, TILE, d), tok)], [h]
    n_stream = len(args)
    in_specs = specs + [pl.BlockSpec((1, TILE, sw), tok)] * 4 + [pl.BlockSpec((1, sw), const)] * 2
    in_specs += [pl.BlockSpec((1, TILE, vw), tok)] * 3 + [pl.BlockSpec(gg.shape, const)]
    in_specs += [pl.BlockSpec((1, 6, d), lambda b, j: (jnp.where(j + skip == 0, ctx_row, b), 0, 0)),
                 resident(wo), pl.BlockSpec((1, d), const), resident(w1), resident(w2)]
    args = args + [yf, yb, xbc, z, sg, dexp, of, ob, r, gg, mod, wo, g2, w1, w2]
    if final:
        in_specs.append(pl.BlockSpec((1, d), const))
        args.append(final_gain)
    return pl.pallas_call(
        functools.partial(_outmlp_kernel, n_stream=n_stream, ff_chunk=1024, final=final),
        grid=(bsz, nt - skip),
        in_specs=in_specs,
        out_specs=pl.BlockSpec((1, TILE, d), lambda b, j: (b, j, 0)),
        out_shape=jax.ShapeDtypeStruct((bsz, t - skip * TILE, d), f32),
        compiler_params=_cparams("parallel", "arbitrary"),
        name="outmlp_final" if final else "outmlp",
    )(*args)


def kernel(x, c, ctx, c_ctx, w_ada, b_ada, norm1_g, w_in, conv_w, conv_b, dt_bias, a_log, d_skip,
           ssd_norm_g, gla_w2, gla_b2, gla_norm_g, w_out, norm2_g, w_ff1, w_ff2, final_norm_g):
    bsz, n_lat, d = x.shape
    ctx_len = ctx.shape[1]
    t = ctx_len + n_lat
    depth = w_in.shape[0]
    ssd_w = ssd_norm_g.shape[-1]
    ssd_heads = dt_bias.shape[-1]
    gla_kw = gla_w2.shape[-1]
    gla_dv = gla_norm_g.shape[-1]
    gla_vw = GLA_HEADS * gla_dv
    bc_w = SSD_GROUPS * SSD_STATE
    rank = GLA_GATE_RANK
    assert ctx_len == TILE and n_lat % TILE == 0 and TILE % GRID_W == 0 and n_lat > TILE
    assert ssd_w == ssd_heads * SSD_HEAD_DIM and 2 * ssd_heads + 2 * rank <= LANES
    assert w_in.shape[-1] == 2 * ssd_w + 2 * bc_w + 2 * ssd_heads + 2 * gla_kw + 2 * gla_vw + 2 * rank

    n_rows = -(-(bsz + 1) // SUBLANES) * SUBLANES
    cc = jnp.zeros((n_rows, d), f32).at[:bsz].set(c).at[bsz].set(c_ctx)
    mods = _ada(cc, w_ada, b_ada).reshape(depth, n_rows, 6, d)

    o_dt = 2 * ssd_w + 2 * bc_w
    o_q = o_dt + 2 * ssd_heads
    o_gate = o_q + 2 * gla_kw + 2 * gla_vw
    widths = (ssd_w, ssd_w + 2 * bc_w, 2 * gla_kw, gla_vw, gla_vw, LANES)
    w_in_t = jnp.swapaxes(w_in, 1, 2)
    segments = ((0, o_dt), (o_q, o_gate), (o_dt, o_q), (o_gate, o_gate + 2 * rank))

    def expander(lane_off):
        lanes = jnp.arange(LANES)[:, None]
        heads = jnp.arange(ssd_w)[None, :] // SSD_HEAD_DIM
        e = (lanes == heads + lane_off).astype(bf16)
        return jnp.concatenate([e, e], axis=0)

    def lane_row(vals, off):
        return jnp.zeros((LANES,), f32).at[off:off + vals.shape[0]].set(vals)

    h = (ctx, x)
    for l in range(depth):
        z, xbc, qk, v, r, small = _inproj(h, mods[l], norm1_g[l][None], w_in_t, l, segments,
                                          conv_w[l].reshape(CONV_K * CONV_K, -1), conv_b[l][None],
                                          widths, bsz, bsz, t, GRID_W)

        pvecs, ees = [], []
        for dirn in range(2):
            off = dirn * ssd_heads
            pvec = jnp.zeros((SUBLANES, LANES), f32)
            pvec = pvec.at[0].set(lane_row(dt_bias[l, dirn], off)).at[1].set(lane_row(a_log[l, dirn], off))
            pvecs.append(pvec.at[2].set(lane_row(jnp.ones((ssd_heads,), f32), off)))
            ees.append(expander(off))
        y_f, y_b = _ssd(xbc, small, pvecs, ees, ssd_w, ssd_heads)

        w2ps = []
        for dirn in range(2):
            off = 2 * ssd_heads + dirn * rank
            w2ps.append(jnp.zeros((LANES, gla_kw), f32).at[off:off + rank].set(gla_w2[l, dirn]).astype(bf16))
        o_f, o_b = _gla(qk, v, small, w2ps, [gla_b2[l, 0][None], gla_b2[l, 1][None]])

        last = l == depth - 1
        dexp = jnp.repeat(d_skip[l, 0] + d_skip[l, 1], SSD_HEAD_DIM)[None]
        h = _outmlp(h, (y_f, y_b, xbc, z, ssd_norm_g[l][None], dexp), (o_f, o_b, r, gla_norm_g[l][None]),
                    mods[l], w_out[l].astype(bf16), norm2_g[l][None],
                    w_ff1[l].astype(bf16), w_ff2[l].astype(bf16), bsz,
                    final_norm_g[None] if last else None)
    return h
```

```python
import functools

import jax
import jax.numpy as jnp
from jax import lax
from jax.experimental import pallas as pl
from jax.experimental.pallas import tpu as pltpu

f32 = jnp.float32
bf16 = jnp.bfloat16

GRID_W = 64
SSD_HEAD_DIM = 64
SSD_GROUPS = 2
SSD_STATE = 128
SSD_CHUNK = 128
CONV_K = 3
GLA_HEADS = 4
GLA_GATE_RANK = 16
GLA_GATE_NORM = 16.0
GLA_CHUNK = 64
EPS = 1e-6

TILE = 256
LANES = 128
SUBLANES = 8
VMEM_LIMIT = 56 * 1024 * 1024
LOG2E = 1.4426950408889634


def _cparams(*sem):
    return pltpu.CompilerParams(dimension_semantics=sem, vmem_limit_bytes=VMEM_LIMIT)


def _softplus(x):
    return jnp.maximum(x, 0.0) + jnp.log(1.0 + jnp.exp2(-LOG2E * jnp.abs(x)))


def _silu(x):
    if x.dtype == bf16:
        return x / (1.0 + jnp.exp(-x))
    return x / (1.0 + jnp.exp2(-LOG2E * x))


def _split_dot(lhs_bf16, x):
    hi = x.astype(bf16)
    lo = (x - hi.astype(f32)).astype(bf16)
    return jnp.dot(jnp.concatenate([lhs_bf16, lhs_bf16], axis=1), jnp.concatenate([hi, lo], axis=0),
                   preferred_element_type=f32)


def _dot_nt(a, b):
    return lax.dot_general(a, b, (((1,), (1,)), ((), ())), preferred_element_type=f32)


def _dot_tn(a, b):
    return lax.dot_general(a, b, (((0,), (0,)), ((), ())), preferred_element_type=f32)


def _chunk_masks(chunk, reverse):
    ii = lax.broadcasted_iota(jnp.int32, (TILE, TILE), 0)
    jj = lax.broadcasted_iota(jnp.int32, (TILE, TILE), 1)
    same = (ii // chunk) == (jj // chunk)
    return same & ((jj >= ii) if reverse else (jj <= ii))


def _chunk_ends(cum, chunk, reverse):
    n = TILE // chunk
    return [cum[ci * chunk:ci * chunk + 1] if reverse else cum[(ci + 1) * chunk - 1:(ci + 1) * chunk]
            for ci in range(n)]


def _interleave(*streams):
    live = list(streams)
    while live:
        for s in list(live):
            if next(s, StopIteration) is StopIteration:
                live.remove(s)


def _ada_kernel(cc_ref, w_ref, b_ref, o_ref):
    s = _silu(cc_ref[...]).astype(bf16)
    o_ref[0] = jnp.dot(s, w_ref[0].astype(bf16), preferred_element_type=f32) + b_ref[0]


def _ada(cc, w_ada, b_ada):
    depth, d, n = w_ada.shape
    tn = n // 4
    rows = cc.shape[0]
    return pl.pallas_call(
        _ada_kernel,
        grid=(depth, n // tn),
        in_specs=[pl.BlockSpec((rows, d), lambda l, j: (0, 0)),
                  pl.BlockSpec((1, d, tn), lambda l, j: (l, 0, j)),
                  pl.BlockSpec((1, 1, tn), lambda l, j: (l, 0, j))],
        out_specs=pl.BlockSpec((1, rows, tn), lambda l, j: (l, 0, j)),
        out_shape=jax.ShapeDtypeStruct((depth, rows, n), f32),
        compiler_params=_cparams("arbitrary", "arbitrary"),
        name="ada",
    )(cc, w_ada, b_ada.reshape(depth, 1, n))


def _stream_specs(h, d):
    if isinstance(h, tuple):
        ctx, x = h
        return ([pl.BlockSpec((1, TILE, d), lambda b, j: (b, 0, 0)),
                 pl.BlockSpec((1, TILE, d), lambda b, j: (b, jnp.maximum(j - 1, 0), 0))], [ctx, x])
    return [pl.BlockSpec((1, TILE, d), lambda b, j: (b, j, 0))], [h]


def _read_stream(refs):
    if len(refs) == 2:
        return jnp.where(pl.program_id(1) == 0, refs[0][0], refs[1][0])
    return refs[0][0]


def _inproj_kernel(*refs, n_stream, cols, nt, segments):
    up_ref, dn_ref, mod_ref, g_ref, win_ref, cw_ref, cb_ref = refs[n_stream:n_stream + 7]
    z_ref, xbc_ref, qk_ref, v_ref, r_ref, sm_ref, w_ref = refs[n_stream + 7:]
    j = pl.program_id(1)

    @pl.when((pl.program_id(0) == 0) & (j == 0))
    def _():
        dst = 0
        for lo, hi in segments:
            w_ref[dst:dst + hi - lo, :] = win_ref[0, lo:hi, :].astype(bf16)
            dst += hi - lo
        w_ref[dst:, :] = jnp.zeros((w_ref.shape[0] - dst, w_ref.shape[1]), bf16)
    m = mod_ref[0]
    gain = g_ref[...] * (1.0 + m[1:2])

    def normed(x):
        ms = jnp.mean(x * x, axis=-1, keepdims=True)
        return (x * lax.rsqrt(ms + EPS) * gain + m[0:1]).astype(bf16)

    u = normed(_read_stream(refs[:n_stream]))
    u_ext = jnp.concatenate([normed(up_ref[0]), u, normed(dn_ref[0])], axis=0)

    zw = z_ref.shape[-1]
    cw = xbc_ref.shape[-1]
    xe = _dot_nt(u_ext, w_ref[zw:zw + cw, :])

    is_ctx = j == 0
    lat = jnp.where(is_ctx, 0.0, 1.0)
    top = jnp.where(j <= 1, 0.0, 1.0)
    bot = jnp.where(is_ctx | (j == nt - 1), 0.0, 1.0)
    w = cw_ref[...]
    rows_per_tile = TILE // cols
    parts = []
    for kj in range(CONV_K):
        blocks = []
        for lr in range(rows_per_tile):
            f_up = top if lr == 0 else lat
            f_dn = bot if lr == rows_per_tile - 1 else lat
            blocks.append(xe[lr * cols:(lr + 1) * cols] * (w[kj:kj + 1] * f_up)
                          + xe[(lr + 1) * cols:(lr + 2) * cols] * w[CONV_K + kj:CONV_K + kj + 1]
                          + xe[(lr + 2) * cols:(lr + 3) * cols] * (w[2 * CONV_K + kj:2 * CONV_K + kj + 1] * f_dn))
        parts.append(jnp.concatenate(blocks, axis=0))

    sub = lax.broadcasted_iota(jnp.int32, (SUBLANES, cw), 0)
    rl = pltpu.roll(parts[0], 1, 0)
    rr = pltpu.roll(parts[2], TILE - 1, 0)
    out_blocks = []
    for t0 in range(0, TILE, SUBLANES):
        lb = rl[t0:t0 + SUBLANES]
        rb = rr[t0:t0 + SUBLANES]
        if t0 % cols == 0:
            keep = (sub >= 1) if t0 == 0 else ((sub >= 1) | is_ctx)
            lb = jnp.where(keep, lb, 0.0)
        if (t0 + SUBLANES) % cols == 0:
            keep = (sub <= SUBLANES - 2) if t0 + SUBLANES == TILE else ((sub <= SUBLANES - 2) | is_ctx)
            rb = jnp.where(keep, rb, 0.0)
        out_blocks.append(lb + rb)
    conv = jnp.concatenate(out_blocks, axis=0) + parts[1] + cb_ref[...]
    xbc_ref[0] = _silu(conv).astype(xbc_ref.dtype)

    col = 0
    for ref in (z_ref, xbc_ref, qk_ref, v_ref, r_ref, sm_ref):
        width = ref.shape[-1]
        if ref is not xbc_ref:
            ref[0] = _dot_nt(u, w_ref[col:col + width, :]).astype(ref.dtype)
        col += width


def _inproj(h, mod, g, w_in, layer, segments, cw9, cbias, widths, ctx_row, bsz, t, cols):
    d = w_in.shape[2]
    nt = t // TILE
    rpt = TILE // cols
    tok = lambda b, j: (b, j, 0)
    specs, args = _stream_specs(h, d)
    if isinstance(h, tuple):
        lat, first, last = h[1], 0, (t - TILE) // cols - 1
    else:
        lat, first, last = h, rpt, t // cols - 1
    up = lambda b, j: (b, jnp.clip(first + (j - 1) * rpt - 1, first, last), 0)
    dn = lambda b, j: (b, jnp.clip(first + j * rpt, first, last), 0)
    const = lambda b, j: (0, 0)
    out_shape = [jax.ShapeDtypeStruct((bsz, t, wd), bf16) for wd in widths[:-1]]
    out_shape.append(jax.ShapeDtypeStruct((bsz, t, widths[-1]), f32))
    return pl.pallas_call(
        functools.partial(_inproj_kernel, n_stream=len(args), cols=cols, nt=nt, segments=segments),
        grid=(bsz, nt),
        in_specs=specs + [pl.BlockSpec((1, cols, d), up), pl.BlockSpec((1, cols, d), dn),
                          pl.BlockSpec((1, 6, d), lambda b, j: (jnp.where(j == 0, ctx_row, b), 0, 0)),
                          pl.BlockSpec((1, d), const),
                          pl.BlockSpec((1,) + w_in.shape[1:], lambda b, j: (layer, 0, 0),
                                       pipeline_mode=pl.Buffered(1)),
                          pl.BlockSpec(cw9.shape, const),
                          pl.BlockSpec(cbias.shape, const)],
        out_specs=[pl.BlockSpec((1, TILE, wd), tok) for wd in widths],
        out_shape=out_shape,
        scratch_shapes=[pltpu.VMEM((sum(widths), d), bf16)],
        compiler_params=_cparams("arbitrary", "arbitrary"),
        name="inproj",
    )(*args, lat, lat, mod, g, w_in, cw9, cbias)


def _ssd_tile(x_ref, b_ref, c_ref, sm_ref, pv_ref, ee_ref, o_ref, state_ref, *, reverse, lane_off):
    q = SSD_CHUNK
    n = SSD_STATE
    gw = state_ref.shape[-1]
    pairs_per_group = gw // LANES
    hd = SSD_HEAD_DIM

    tri = _chunk_masks(q, reverse).astype(bf16)
    ci_ = lax.broadcasted_iota(jnp.int32, (q, q), 0)
    cj_ = lax.broadcasted_iota(jnp.int32, (q, q), 1)
    mask = (cj_ >= ci_) if reverse else (cj_ <= ci_)
    lo_half = lax.broadcasted_iota(jnp.int32, (q, LANES), 1) < hd
    bias = pv_ref[0:1, :]
    a_coef = -jnp.exp(pv_ref[1:2, :]) * pv_ref[2:3, :] * LOG2E
    ee = ee_ref[...]

    def expand(v):
        hi = v.astype(bf16)
        lo = (v - hi.astype(f32)).astype(bf16)
        return jnp.dot(jnp.concatenate([hi, lo], axis=1), ee, preferred_element_type=f32)

    dt = _softplus(sm_ref[0] + bias)
    ac = _split_dot(tri, dt * a_coef)
    ends = _chunk_ends(ac, q, reverse)
    ac_end = jnp.concatenate([jnp.broadcast_to(e, (q, LANES)) for e in ends], axis=0)
    wexp = expand(dt * jnp.exp2(ac_end - ac)).astype(bf16)
    eexp = expand(jnp.exp2(ac))
    src_t = (ac - jnp.log2(dt)).T
    yield

    chunks = range(TILE // q)
    for ci in (reversed(chunks) if reverse else chunks):
        rs = slice(ci * q, (ci + 1) * q)
        end_row = ci * q if reverse else (ci + 1) * q - 1
        xs = x_ref[0, rs, :]
        xw = xs * wexp[rs]
        for g in range(SSD_GROUPS):
            gs = slice(g * gw, (g + 1) * gw)
            bg = b_ref[0, rs, g * n:(g + 1) * n]
            cg = c_ref[0, rs, g * n:(g + 1) * n]
            cbm16 = _dot_nt(cg, bg).astype(bf16)
            s_g = state_ref[g]
            y_off = jnp.dot(cg, s_g.astype(bf16), preferred_element_type=f32) * eexp[rs, gs]
            ys = []
            for pp in range(pairs_per_group):
                p = g * pairs_per_group + pp
                pieces = []
                for h in (2 * p, 2 * p + 1):
                    li = lane_off + h
                    colb = jnp.broadcast_to(ac[rs, li:li + 1], (q, q))
                    dec = jnp.exp2(jnp.where(mask, colb - src_t[li:li + 1, rs], -jnp.inf))
                    pieces.append(cbm16 * dec.astype(bf16))
                lhs = jnp.concatenate(pieces, axis=1)
                xp = xs[:, p * LANES:(p + 1) * LANES]
                zero = jnp.zeros_like(xp)
                rhs = jnp.concatenate([jnp.where(lo_half, xp, zero), jnp.where(lo_half, zero, xp)], axis=0)
                ys.append(jnp.dot(lhs, rhs, preferred_element_type=f32))
                yield
            state_ref[g] = s_g * eexp[end_row:end_row + 1, gs] + _dot_tn(bg, xw[:, gs])
            o_ref[0, rs, gs] = (jnp.concatenate(ys, axis=1) + y_off).astype(o_ref.dtype)
            yield


def _ssd_kernel(xf_ref, bf_ref, cf_ref, smf_ref, xb_ref, bb_ref, cb_ref, smb_ref,
                pvf_ref, pvb_ref, eef_ref, eeb_ref, of_ref, ob_ref, state_ref, *, heads):
    @pl.when(pl.program_id(1) == 0)
    def _():
        state_ref[...] = jnp.zeros_like(state_ref)

    _interleave(_ssd_tile(xf_ref, bf_ref, cf_ref, smf_ref, pvf_ref, eef_ref, of_ref, state_ref.at[0],
                          reverse=False, lane_off=0),
                _ssd_tile(xb_ref, bb_ref, cb_ref, smb_ref, pvb_ref, eeb_ref, ob_ref, state_ref.at[1],
                          reverse=True, lane_off=heads))


def _scan_tile(nt, reverse):
    if reverse:
        return lambda s: jnp.where(s == 0, 0, nt - s)
    return lambda s: s


def _ssd(xbc, small, pvecs, ees, width, heads):
    bsz, t, _ = xbc.shape
    nt = t // TILE
    n = SSD_STATE
    gw = width // SSD_GROUPS
    bcb = width // (SSD_GROUPS * n)
    const = lambda b, s: (0, 0)
    in_specs, args = [], []
    for reverse in (False, True):
        tile = _scan_tile(nt, reverse)
        tok = lambda cb, tile=tile: (lambda b, s: (b, tile(s), cb))
        in_specs += [pl.BlockSpec((1, TILE, width), tok(0)),
                     pl.BlockSpec((1, TILE, SSD_GROUPS * n), tok(bcb)),
                     pl.BlockSpec((1, TILE, SSD_GROUPS * n), tok(bcb + 1)),
                     pl.BlockSpec((1, TILE, LANES), tok(0))]
        args += [xbc, xbc, xbc, small]
    in_specs += [pl.BlockSpec(a.shape, const) for a in (*pvecs, *ees)]
    args += [*pvecs, *ees]
    out_specs = [pl.BlockSpec((1, TILE, width), lambda b, s, tile=_scan_tile(nt, rev): (b, tile(s), 0))
                 for rev in (False, True)]
    return pl.pallas_call(
        functools.partial(_ssd_kernel, heads=heads),
        grid=(bsz, nt),
        in_specs=in_specs,
        out_specs=out_specs,
        out_shape=[jax.ShapeDtypeStruct((bsz, t, width), bf16)] * 2,
        scratch_shapes=[pltpu.VMEM((2, SSD_GROUPS, n, gw), f32)],
        compiler_params=_cparams("parallel", "arbitrary"),
        name="ssd",
    )(*args)


def _gla_tile(q_ref, k_ref, v_ref, sm_ref, w2_ref, b2_ref, o_ref, state_ref, *, reverse):
    c = GLA_CHUNK
    dk = state_ref.shape[1]
    dv = state_ref.shape[2]
    kw = GLA_HEADS * dk
    scale = dk ** -0.5

    mask = _chunk_masks(c, reverse)
    tri = mask.astype(bf16)

    gate = jnp.dot(sm_ref[0].astype(bf16), w2_ref[...], preferred_element_type=f32) + b2_ref[...]
    log_a = -_softplus(-gate) * (LOG2E / GLA_GATE_NORM)
    bc = _split_dot(tri, log_a)
    yield
    ends = _chunk_ends(bc, c, reverse)
    b_end = jnp.concatenate([jnp.broadcast_to(e, (c, kw)) for e in ends], axis=0)
    q_dec = q_ref[0] * (jnp.exp2(bc) * scale).astype(bf16)
    k16 = k_ref[0]
    k_inv = k16 * jnp.exp2(-bc).astype(bf16)
    k_end = k16 * jnp.exp2(b_end - bc).astype(bf16)
    dec_cols = [jnp.broadcast_to(jnp.exp2(e), (LANES, kw)).T for e in ends]
    yield

    chunks = range(TILE // c)
    dks = [slice(h * dk, (h + 1) * dk) for h in range(GLA_HEADS)]
    dvs = [slice(h * dv, (h + 1) * dv) for h in range(GLA_HEADS)]
    o_intra = []
    for h in range(GLA_HEADS):
        att = jnp.where(mask, _dot_nt(q_dec[:, dks[h]], k_inv[:, dks[h]]), 0.0).astype(bf16)
        o_intra.append(jnp.dot(att, v_ref[0, :, dvs[h]], preferred_element_type=f32))
        yield
    states = [state_ref[h] for h in range(GLA_HEADS)]
    for ci in (reversed(chunks) if reverse else chunks):
        rs = slice(ci * c, (ci + 1) * c)
        for h in range(GLA_HEADS):
            s = states[h]
            o = o_intra[h][rs] + jnp.dot(q_dec[rs, dks[h]], s.astype(bf16), preferred_element_type=f32)
            o_ref[0, rs, dvs[h]] = o.astype(o_ref.dtype)
            dec = dec_cols[ci][dks[h], :]
            states[h] = (s * jnp.concatenate([dec] * (dv // LANES), axis=1)
                         + _dot_tn(k_end[rs, dks[h]], v_ref[0, rs, dvs[h]]))
            yield
    for h in range(GLA_HEADS):
        state_ref[h] = states[h]


def _gla_kernel(qf_ref, kf_ref, vf_ref, smf_ref, qb_ref, kb_ref, vb_ref, smb_ref,
                w2f_ref, w2b_ref, b2f_ref, b2b_ref, of_ref, ob_ref, state_ref):
    @pl.when(pl.program_id(1) == 0)
    def _():
        state_ref[...] = jnp.zeros_like(state_ref)

    _interleave(_gla_tile(qf_ref, kf_ref, vf_ref, smf_ref, w2f_ref, b2f_ref, of_ref, state_ref.at[0], reverse=False),
                _gla_tile(qb_ref, kb_ref, vb_ref, smb_ref, w2b_ref, b2b_ref, ob_ref, state_ref.at[1], reverse=True))


def _gla(qk, v, small, w2ps, b2s):
    bsz, t, kw2 = qk.shape
    kw = kw2 // 2
    vw = v.shape[-1]
    nt = t // TILE
    const = lambda b, s: (0, 0)
    in_specs, args = [], []
    for reverse in (False, True):
        tile = _scan_tile(nt, reverse)
        tok = lambda cb, tile=tile: (lambda b, s: (b, tile(s), cb))
        in_specs += [pl.BlockSpec((1, TILE, kw), tok(0)),
                     pl.BlockSpec((1, TILE, kw), tok(1)),
                     pl.BlockSpec((1, TILE, vw), tok(0)),
                     pl.BlockSpec((1, TILE, LANES), tok(0))]
        args += [qk, qk, v, small]
    in_specs += [pl.BlockSpec(a.shape, const) for a in (*w2ps, *b2s)]
    args += [*w2ps, *b2s]
    out_specs = [pl.BlockSpec((1, TILE, vw), lambda b, s, tile=_scan_tile(nt, rev): (b, tile(s), 0))
                 for rev in (False, True)]
    return pl.pallas_call(
        _gla_kernel,
        grid=(bsz, nt),
        in_specs=in_specs,
        out_specs=out_specs,
        out_shape=[jax.ShapeDtypeStruct((bsz, t, vw), bf16)] * 2,
        scratch_shapes=[pltpu.VMEM((2, GLA_HEADS, kw // GLA_HEADS, vw // GLA_HEADS), f32)],
        compiler_params=_cparams("parallel", "arbitrary"),
        name="gla",
    )(*args)


def _outmlp_kernel(*refs, n_stream, ff_chunk, final):
    (yf_ref, yb_ref, xs_ref, z_ref, sg_ref, dexp_ref, of_ref, ob_ref, r_ref, gg_ref,
     mod_ref, wo_ref, g2_ref, w1_ref, w2_ref) = refs[n_stream:n_stream + 15]
    o_ref = refs[-1]
    m = mod_ref[0]

    sw = yf_ref.shape[-1]
    gw = sw // SSD_GROUPS
    y = (yf_ref[0] + yb_ref[0]).astype(f32) + dexp_ref[...] * xs_ref[0].astype(f32)
    y = y * _silu(z_ref[0]).astype(f32)
    mix = None
    for g in range(SSD_GROUPS):
        yg = y[:, g * gw:(g + 1) * gw]
        ms = jnp.mean(yg * yg, axis=-1, keepdims=True)
        part = (yg * lax.rsqrt(ms + EPS) * sg_ref[:, g * gw:(g + 1) * gw]).astype(bf16)
        term = jnp.dot(part, wo_ref[g * gw:(g + 1) * gw, :], preferred_element_type=f32)
        mix = term if mix is None else mix + term
    dv = gg_ref.shape[-1]
    o = (of_ref[0] + ob_ref[0]).astype(f32)
    gate = _silu(r_ref[0]).astype(f32)
    for hh in range(GLA_HEADS):
        oh = o[:, hh * dv:(hh + 1) * dv]
        ms = jnp.mean(oh * oh, axis=-1, keepdims=True)
        part = (oh * lax.rsqrt(ms + EPS) * gg_ref[...] * gate[:, hh * dv:(hh + 1) * dv]).astype(bf16)
        mix = mix + jnp.dot(part, wo_ref[sw + hh * dv:sw + (hh + 1) * dv, :], preferred_element_type=f32)
    h1 = _read_stream(refs[:n_stream]) + m[2:3] * mix
    ms = jnp.mean(h1 * h1, axis=-1, keepdims=True)
    u2 = (h1 * lax.rsqrt(ms + EPS) * (g2_ref[...] * (1.0 + m[4:5])) + m[3:4]).astype(bf16)
    acc = jnp.zeros_like(h1)
    for c0 in range(0, w1_ref.shape[-1], ff_chunk):
        hid = jnp.maximum(jnp.dot(u2, w1_ref[:, c0:c0 + ff_chunk], preferred_element_type=f32), 0.0)
        acc = acc + jnp.dot((hid * hid).astype(bf16), w2_ref[c0:c0 + ff_chunk, :], preferred_element_type=f32)
    h2 = h1 + m[5:6] * acc
    if final:
        fg_ref = refs[n_stream + 15]
        ms2 = jnp.mean(h2 * h2, axis=-1, keepdims=True)
        h2 = h2 * lax.rsqrt(ms2 + EPS) * fg_ref[...]
    o_ref[0] = h2


def _outmlp(h, ssd_args, gla_args, mod, wo, g2, w1, w2, ctx_row, final_gain=None):
    yf, yb, xbc, z, sg, dexp = ssd_args
    of, ob, r, gg = gla_args
    bsz, t, sw = yf.shape
    vw = of.shape[-1]
    d = wo.shape[-1]
    nt = t // TILE
    final = final_gain is not None
    skip = 1 if final else 0
    assert not (final and isinstance(h, tuple))
    tok = lambda b, j: (b, j + skip, 0)
    const = lambda b, j: (0, 0)
    resident = lambda a: pl.BlockSpec(a.shape, const, pipeline_mode=pl.Buffered(1))
    if isinstance(h, tuple):
        specs, args = _stream_specs(h, d)
    else:
        specs, args = [pl.BlockSpec((1, TILE, d), tok)], [h]
    n_stream = len(args)
    in_specs = specs + [pl.BlockSpec((1, TILE, sw), tok)] * 4 + [pl.BlockSpec((1, sw), const)] * 2
    in_specs += [pl.BlockSpec((1, TILE, vw), tok)] * 3 + [pl.BlockSpec(gg.shape, const)]
    in_specs += [pl.BlockSpec((1, 6, d), lambda b, j: (jnp.where(j + skip == 0, ctx_row, b), 0, 0)),
                 resident(wo), pl.BlockSpec((1, d), const), resident(w1), resident(w2)]
    args = args + [yf, yb, xbc, z, sg, dexp, of, ob, r, gg, mod, wo, g2, w1, w2]
    if final:
        in_specs.append(pl.BlockSpec((1, d), const))
        args.append(final_gain)
    return pl.pallas_call(
        functools.partial(_outmlp_kernel, n_stream=n_stream, ff_chunk=1024, final=final),
        grid=(bsz, nt - skip),
        in_specs=in_specs,
        out_specs=pl.BlockSpec((1, TILE, d), lambda b, j: (b, j, 0)),
        out_shape=jax.ShapeDtypeStruct((bsz, t - skip * TILE, d), f32),
        compiler_params=_cparams("parallel", "arbitrary"),
        name="outmlp_final" if final else "outmlp",
    )(*args)


def kernel(x, c, ctx, c_ctx, w_ada, b_ada, norm1_g, w_in, conv_w, conv_b, dt_bias, a_log, d_skip,
           ssd_norm_g, gla_w2, gla_b2, gla_norm_g, w_out, norm2_g, w_ff1, w_ff2, final_norm_g):
    bsz, n_lat, d = x.shape
    ctx_len = ctx.shape[1]
    t = ctx_len + n_lat
    depth = w_in.shape[0]
    ssd_w = ssd_norm_g.shape[-1]
    ssd_heads = dt_bias.shape[-1]
    gla_kw = gla_w2.shape[-1]
    gla_dv = gla_norm_g.shape[-1]
    gla_vw = GLA_HEADS * gla_dv
    bc_w = SSD_GROUPS * SSD_STATE
    rank = GLA_GATE_RANK
    assert ctx_len == TILE and n_lat % TILE == 0 and TILE % GRID_W == 0 and n_lat > TILE
    assert ssd_w == ssd_heads * SSD_HEAD_DIM and 2 * ssd_heads + 2 * rank <= LANES
    assert w_in.shape[-1] == 2 * ssd_w + 2 * bc_w + 2 * ssd_heads + 2 * gla_kw + 2 * gla_vw + 2 * rank

    n_rows = -(-(bsz + 1) // SUBLANES) * SUBLANES
    cc = jnp.zeros((n_rows, d), f32).at[:bsz].set(c).at[bsz].set(c_ctx)
    mods = _ada(cc, w_ada, b_ada).reshape(depth, n_rows, 6, d)

    o_dt = 2 * ssd_w + 2 * bc_w
    o_q = o_dt + 2 * ssd_heads
    o_gate = o_q + 2 * gla_kw + 2 * gla_vw
    widths = (ssd_w, ssd_w + 2 * bc_w, 2 * gla_kw, gla_vw, gla_vw, LANES)
    w_in_t = jnp.swapaxes(w_in, 1, 2)
    segments = ((0, o_dt), (o_q, o_gate), (o_dt, o_q), (o_gate, o_gate + 2 * rank))

    def expander(lane_off):
        lanes = jnp.arange(LANES)[:, None]
        heads = jnp.arange(ssd_w)[None, :] // SSD_HEAD_DIM
        e = (lanes == heads + lane_off).astype(bf16)
        return jnp.concatenate([e, e], axis=0)

    def lane_row(vals, off):
        return jnp.zeros((LANES,), f32).at[off:off + vals.shape[0]].set(vals)

    h = (ctx, x)
    for l in range(depth):
        z, xbc, qk, v, r, small = _inproj(h, mods[l], norm1_g[l][None], w_in_t, l, segments,
                                          conv_w[l].reshape(CONV_K * CONV_K, -1), conv_b[l][None],
                                          widths, bsz, bsz, t, GRID_W)

        pvecs, ees = [], []
        for dirn in range(2):
            off = dirn * ssd_heads
            pvec = jnp.zeros((SUBLANES, LANES), f32)
            pvec = pvec.at[0].set(lane_row(dt_bias[l, dirn], off)).at[1].set(lane_row(a_log[l, dirn], off))
            pvecs.append(pvec.at[2].set(lane_row(jnp.ones((ssd_heads,), f32), off)))
            ees.append(expander(off))
        y_f, y_b = _ssd(xbc, small, pvecs, ees, ssd_w, ssd_heads)

        w2ps = []
        for dirn in range(2):
            off = 2 * ssd_heads + dirn * rank
            w2ps.append(jnp.zeros((LANES, gla_kw), f32).at[off:off + rank].set(gla_w2[l, dirn]).astype(bf16))
        o_f, o_b = _gla(qk, v, small, w2ps, [gla_b2[l, 0][None], gla_b2[l, 1][None]])

        last = l == depth - 1
        dexp = jnp.repeat(d_skip[l, 0] + d_skip[l, 1], SSD_HEAD_DIM)[None]
        h = _outmlp(h, (y_f, y_b, xbc, z, ssd_norm_g[l][None], dexp), (o_f, o_b, r, gla_norm_g[l][None]),
                    mods[l], w_out[l].astype(bf16), norm2_g[l][None],
                    w_ff1[l].astype(bf16), w_ff2[l].astype(bf16), bsz,
                    final_norm_g[None] if last else None)
    return h
```

```python
import functools

import jax
import jax.numpy as jnp
from jax import lax
from jax.experimental import pallas as pl
from jax.experimental.pallas import tpu as pltpu

f32 = jnp.float32
bf16 = jnp.bfloat16

GRID_W = 64
SSD_HEAD_DIM = 64
SSD_GROUPS = 2
SSD_STATE = 128
SSD_CHUNK = 128
CONV_K = 3
GLA_HEADS = 4
GLA_GATE_RANK = 16
GLA_GATE_NORM = 16.0
GLA_CHUNK = 64
EPS = 1e-6

TILE = 256
LANES = 128
SUBLANES = 8
VMEM_LIMIT = 56 * 1024 * 1024
LOG2E = 1.4426950408889634


def _cparams(*sem):
    return pltpu.CompilerParams(dimension_semantics=sem, vmem_limit_bytes=VMEM_LIMIT)


def _softplus(x):
    return jnp.maximum(x, 0.0) + jnp.log(1.0 + jnp.exp2(-LOG2E * jnp.abs(x)))


def _silu(x):
    if x.dtype == bf16:
        return x / (1.0 + jnp.exp(-x))
    return x / (1.0 + jnp.exp2(-LOG2E * x))


def _split_dot(lhs_bf16, x):
    hi = x.astype(bf16)
    lo = (x - hi.astype(f32)).astype(bf16)
    return jnp.dot(jnp.concatenate([lhs_bf16, lhs_bf16], axis=1), jnp.concatenate([hi, lo], axis=0),
                   preferred_element_type=f32)


def _dot_nt(a, b):
    return lax.dot_general(a, b, (((1,), (1,)), ((), ())), preferred_element_type=f32)


def _dot_tn(a, b):
    return lax.dot_general(a, b, (((0,), (0,)), ((), ())), preferred_element_type=f32)


def _chunk_masks(chunk, reverse):
    ii = lax.broadcasted_iota(jnp.int32, (TILE, TILE), 0)
    jj = lax.broadcasted_iota(jnp.int32, (TILE, TILE), 1)
    same = (ii // chunk) == (jj // chunk)
    return same & ((jj >= ii) if reverse else (jj <= ii))


def _chunk_ends(cum, chunk, reverse):
    n = TILE // chunk
    return [cum[ci * chunk:ci * chunk + 1] if reverse else cum[(ci + 1) * chunk - 1:(ci + 1) * chunk]
            for ci in range(n)]


def _interleave(*streams):
    live = list(streams)
    while live:
        for s in list(live):
            if next(s, StopIteration) is StopIteration:
                live.remove(s)


def _ada_kernel(cc_ref, w_ref, b_ref, o_ref):
    s = _silu(cc_ref[...]).astype(bf16)
    o_ref[0] = jnp.dot(s, w_ref[0].astype(bf16), preferred_element_type=f32) + b_ref[0]


def _ada(cc, w_ada, b_ada):
    depth, d, n = w_ada.shape
    tn = n // 4
    rows = cc.shape[0]
    return pl.pallas_call(
        _ada_kernel,
        grid=(depth, n // tn),
        in_specs=[pl.BlockSpec((rows, d), lambda l, j: (0, 0)),
                  pl.BlockSpec((1, d, tn), lambda l, j: (l, 0, j)),
                  pl.BlockSpec((1, 1, tn), lambda l, j: (l, 0, j))],
        out_specs=pl.BlockSpec((1, rows, tn), lambda l, j: (l, 0, j)),
        out_shape=jax.ShapeDtypeStruct((depth, rows, n), f32),
        compiler_params=_cparams("arbitrary", "arbitrary"),
        name="ada",
    )(cc, w_ada, b_ada.reshape(depth, 1, n))


def _stream_specs(h, d):
    if isinstance(h, tuple):
        ctx, x = h
        return ([pl.BlockSpec((1, TILE, d), lambda b, j: (b, 0, 0)),
                 pl.BlockSpec((1, TILE, d), lambda b, j: (b, jnp.maximum(j - 1, 0), 0))], [ctx, x])
    return [pl.BlockSpec((1, TILE, d), lambda b, j: (b, j, 0))], [h]


def _read_stream(refs):
    if len(refs) == 2:
        return jnp.where(pl.program_id(1) == 0, refs[0][0], refs[1][0])
    return refs[0][0]


def _inproj_kernel(*refs, n_stream, cols, nt, segments):
    up_ref, dn_ref, mod_ref, g_ref, win_ref, cw_ref, cb_ref = refs[n_stream:n_stream + 7]
    z_ref, xbc_ref, qk_ref, v_ref, r_ref, sm_ref, w_ref = refs[n_stream + 7:]
    j = pl.program_id(1)

    @pl.when((pl.program_id(0) == 0) & (j == 0))
    def _():
        dst = 0
        for lo, hi in segments:
            w_ref[dst:dst + hi - lo, :] = win_ref[0, lo:hi, :].astype(bf16)
            dst += hi - lo
        w_ref[dst:, :] = jnp.zeros((w_ref.shape[0] - dst, w_ref.shape[1]), bf16)
    m = mod_ref[0]
    gain = g_ref[...] * (1.0 + m[1:2])

    def normed(x):
        ms = jnp.mean(x * x, axis=-1, keepdims=True)
        return (x * lax.rsqrt(ms + EPS) * gain + m[0:1]).astype(bf16)

    u = normed(_read_stream(refs[:n_stream]))
    u_ext = jnp.concatenate([normed(up_ref[0]), u, normed(dn_ref[0])], axis=0)

    zw = z_ref.shape[-1]
    cw = xbc_ref.shape[-1]
    xe = _dot_nt(u_ext, w_ref[zw:zw + cw, :])

    is_ctx = j == 0
    lat = jnp.where(is_ctx, 0.0, 1.0)
    top = jnp.where(j <= 1, 0.0, 1.0)
    bot = jnp.where(is_ctx | (j == nt - 1), 0.0, 1.0)
    w = cw_ref[...]
    rows_per_tile = TILE // cols
    xe = xe.astype(bf16)
    parts = []
    for kj in range(CONV_K):
        blocks = []
        for lr in range(rows_per_tile):
            f_up = top if lr == 0 else lat
            f_dn = bot if lr == rows_per_tile - 1 else lat
            blocks.append(xe[lr * cols:(lr + 1) * cols] * (w[kj:kj + 1] * f_up).astype(bf16)
                          + xe[(lr + 1) * cols:(lr + 2) * cols] * w[CONV_K + kj:CONV_K + kj + 1].astype(bf16)
                          + xe[(lr + 2) * cols:(lr + 3) * cols] * (w[2 * CONV_K + kj:2 * CONV_K + kj + 1] * f_dn).astype(bf16))
        parts.append(jnp.concatenate(blocks, axis=0).astype(f32))

    sub = lax.broadcasted_iota(jnp.int32, (SUBLANES, cw), 0)
    rl = pltpu.roll(parts[0], 1, 0)
    rr = pltpu.roll(parts[2], TILE - 1, 0)
    out_blocks = []
    for t0 in range(0, TILE, SUBLANES):
        lb = rl[t0:t0 + SUBLANES]
        rb = rr[t0:t0 + SUBLANES]
        if t0 % cols == 0:
            keep = (sub >= 1) if t0 == 0 else ((sub >= 1) | is_ctx)
            lb = jnp.where(keep, lb, 0.0)
        if (t0 + SUBLANES) % cols == 0:
            keep = (sub <= SUBLANES - 2) if t0 + SUBLANES == TILE else ((sub <= SUBLANES - 2) | is_ctx)
            rb = jnp.where(keep, rb, 0.0)
        out_blocks.append(lb + rb)
    conv = jnp.concatenate(out_blocks, axis=0) + parts[1] + cb_ref[...]
    xbc_ref[0] = _silu(conv).astype(xbc_ref.dtype)

    col = 0
    for ref in (z_ref, xbc_ref, qk_ref, v_ref, r_ref, sm_ref):
        width = ref.shape[-1]
        if ref is not xbc_ref:
            ref[0] = _dot_nt(u, w_ref[col:col + width, :]).astype(ref.dtype)
        col += width


def _inproj(h, mod, g, w_in, layer, segments, cw9, cbias, widths, ctx_row, bsz, t, cols):
    d = w_in.shape[2]
    nt = t // TILE
    rpt = TILE // cols
    tok = lambda b, j: (b, j, 0)
    specs, args = _stream_specs(h, d)
    if isinstance(h, tuple):
        lat, first, last = h[1], 0, (t - TILE) // cols - 1
    else:
        lat, first, last = h, rpt, t // cols - 1
    up = lambda b, j: (b, jnp.clip(first + (j - 1) * rpt - 1, first, last), 0)
    dn = lambda b, j: (b, jnp.clip(first + j * rpt, first, last), 0)
    const = lambda b, j: (0, 0)
    out_shape = [jax.ShapeDtypeStruct((bsz, t, wd), bf16) for wd in widths[:-1]]
    out_shape.append(jax.ShapeDtypeStruct((bsz, t, widths[-1]), f32))
    return pl.pallas_call(
        functools.partial(_inproj_kernel, n_stream=len(args), cols=cols, nt=nt, segments=segments),
        grid=(bsz, nt),
        in_specs=specs + [pl.BlockSpec((1, cols, d), up), pl.BlockSpec((1, cols, d), dn),
                          pl.BlockSpec((1, 6, d), lambda b, j: (jnp.where(j == 0, ctx_row, b), 0, 0)),
                          pl.BlockSpec((1, d), const),
                          pl.BlockSpec((1,) + w_in.shape[1:], lambda b, j: (layer, 0, 0),
                                       pipeline_mode=pl.Buffered(1)),
                          pl.BlockSpec(cw9.shape, const),
                          pl.BlockSpec(cbias.shape, const)],
        out_specs=[pl.BlockSpec((1, TILE, wd), tok) for wd in widths],
        out_shape=out_shape,
        scratch_shapes=[pltpu.VMEM((sum(widths), d), bf16)],
        compiler_params=_cparams("arbitrary", "arbitrary"),
        name="inproj",
    )(*args, lat, lat, mod, g, w_in, cw9, cbias)


def _ssd_tile(x_ref, b_ref, c_ref, sm_ref, pv_ref, ee_ref, o_ref, state_ref, *, reverse, lane_off):
    q = SSD_CHUNK
    n = SSD_STATE
    gw = state_ref.shape[-1]
    pairs_per_group = gw // LANES
    hd = SSD_HEAD_DIM

    tri = _chunk_masks(q, reverse).astype(bf16)
    ci_ = lax.broadcasted_iota(jnp.int32, (q, q), 0)
    cj_ = lax.broadcasted_iota(jnp.int32, (q, q), 1)
    mask = (cj_ >= ci_) if reverse else (cj_ <= ci_)
    lo_half = lax.broadcasted_iota(jnp.int32, (q, LANES), 1) < hd
    bias = pv_ref[0:1, :]
    a_coef = -jnp.exp(pv_ref[1:2, :]) * pv_ref[2:3, :] * LOG2E
    ee = ee_ref[...]

    def expand(v):
        hi = v.astype(bf16)
        lo = (v - hi.astype(f32)).astype(bf16)
        return jnp.dot(jnp.concatenate([hi, lo], axis=1), ee, preferred_element_type=f32)

    dt = _softplus(sm_ref[0] + bias)
    ac = _split_dot(tri, dt * a_coef)
    ends = _chunk_ends(ac, q, reverse)
    ac_end = jnp.concatenate([jnp.broadcast_to(e, (q, LANES)) for e in ends], axis=0)
    wexp = expand(dt * jnp.exp2(ac_end - ac)).astype(bf16)
    eexp = expand(jnp.exp2(ac))
    src_t = (ac - jnp.log2(dt)).T
    yield

    chunks = range(TILE // q)
    for ci in (reversed(chunks) if reverse else chunks):
        rs = slice(ci * q, (ci + 1) * q)
        end_row = ci * q if reverse else (ci + 1) * q - 1
        xs = x_ref[0, rs, :]
        xw = xs * wexp[rs]
        for g in range(SSD_GROUPS):
            gs = slice(g * gw, (g + 1) * gw)
            bg = b_ref[0, rs, g * n:(g + 1) * n]
            cg = c_ref[0, rs, g * n:(g + 1) * n]
            cbm16 = _dot_nt(cg, bg).astype(bf16)
            s_g = state_ref[g]
            y_off = jnp.dot(cg, s_g.astype(bf16), preferred_element_type=f32) * eexp[rs, gs]
            ys = []
            for pp in range(pairs_per_group):
                p = g * pairs_per_group + pp
                pieces = []
                for h in (2 * p, 2 * p + 1):
                    li = lane_off + h
                    colb = jnp.broadcast_to(ac[rs, li:li + 1], (q, q))
                    dec = jnp.exp2(jnp.where(mask, colb - src_t[li:li + 1, rs], -jnp.inf))
                    pieces.append(cbm16 * dec.astype(bf16))
                lhs = jnp.concatenate(pieces, axis=1)
                xp = xs[:, p * LANES:(p + 1) * LANES]
                zero = jnp.zeros_like(xp)
                rhs = jnp.concatenate([jnp.where(lo_half, xp, zero), jnp.where(lo_half, zero, xp)], axis=0)
                ys.append(jnp.dot(lhs, rhs, preferred_element_type=f32))
                yield
            state_ref[g] = s_g * eexp[end_row:end_row + 1, gs] + _dot_tn(bg, xw[:, gs])
            o_ref[0, rs, gs] = (jnp.concatenate(ys, axis=1) + y_off).astype(o_ref.dtype)
            yield


def _ssd_kernel(xf_ref, bf_ref, cf_ref, smf_ref, xb_ref, bb_ref, cb_ref, smb_ref,
                pvf_ref, pvb_ref, eef_ref, eeb_ref, of_ref, ob_ref, state_ref, *, heads):
    @pl.when(pl.program_id(1) == 0)
    def _():
        state_ref[...] = jnp.zeros_like(state_ref)

    _interleave(_ssd_tile(xf_ref, bf_ref, cf_ref, smf_ref, pvf_ref, eef_ref, of_ref, state_ref.at[0],
                          reverse=False, lane_off=0),
                _ssd_tile(xb_ref, bb_ref, cb_ref, smb_ref, pvb_ref, eeb_ref, ob_ref, state_ref.at[1],
                          reverse=True, lane_off=heads))


def _scan_tile(nt, reverse):
    if reverse:
        return lambda s: jnp.where(s == 0, 0, nt - s)
    return lambda s: s


def _ssd(xbc, small, pvecs, ees, width, heads):
    bsz, t, _ = xbc.shape
    nt = t // TILE
    n = SSD_STATE
    gw = width // SSD_GROUPS
    bcb = width // (SSD_GROUPS * n)
    const = lambda b, s: (0, 0)
    in_specs, args = [], []
    for reverse in (False, True):
        tile = _scan_tile(nt, reverse)
        tok = lambda cb, tile=tile: (lambda b, s: (b, tile(s), cb))
        in_specs += [pl.BlockSpec((1, TILE, width), tok(0)),
                     pl.BlockSpec((1, TILE, SSD_GROUPS * n), tok(bcb)),
                     pl.BlockSpec((1, TILE, SSD_GROUPS * n), tok(bcb + 1)),
                     pl.BlockSpec((1, TILE, LANES), tok(0))]
        args += [xbc, xbc, xbc, small]
    in_specs += [pl.BlockSpec(a.shape, const) for a in (*pvecs, *ees)]
    args += [*pvecs, *ees]
    out_specs = [pl.BlockSpec((1, TILE, width), lambda b, s, tile=_scan_tile(nt, rev): (b, tile(s), 0))
                 for rev in (False, True)]
    return pl.pallas_call(
        functools.partial(_ssd_kernel, heads=heads),
        grid=(bsz, nt),
        in_specs=in_specs,
        out_specs=out_specs,
        out_shape=[jax.ShapeDtypeStruct((bsz, t, width), bf16)] * 2,
        scratch_shapes=[pltpu.VMEM((2, SSD_GROUPS, n, gw), f32)],
        compiler_params=_cparams("parallel", "arbitrary"),
        name="ssd",
    )(*args)


def _gla_tile(q_ref, k_ref, v_ref, sm_ref, w2_ref, b2_ref, o_ref, state_ref, *, reverse):
    c = GLA_CHUNK
    dk = state_ref.shape[1]
    dv = state_ref.shape[2]
    kw = GLA_HEADS * dk
    scale = dk ** -0.5

    mask = _chunk_masks(c, reverse)
    tri = mask.astype(bf16)

    gate = jnp.dot(sm_ref[0].astype(bf16), w2_ref[...], preferred_element_type=f32) + b2_ref[...]
    log_a = -_softplus(-gate) * (LOG2E / GLA_GATE_NORM)
    bc = _split_dot(tri, log_a)
    yield
    ends = _chunk_ends(bc, c, reverse)
    b_end = jnp.concatenate([jnp.broadcast_to(e, (c, kw)) for e in ends], axis=0)
    q_dec = q_ref[0] * (jnp.exp2(bc) * scale).astype(bf16)
    k16 = k_ref[0]
    k_inv = k16 * jnp.exp2(-bc).astype(bf16)
    k_end = k16 * jnp.exp2(b_end - bc).astype(bf16)
    dec_cols = [jnp.broadcast_to(jnp.exp2(e), (LANES, kw)).T for e in ends]
    yield

    chunks = range(TILE // c)
    dks = [slice(h * dk, (h + 1) * dk) for h in range(GLA_HEADS)]
    dvs = [slice(h * dv, (h + 1) * dv) for h in range(GLA_HEADS)]
    o_intra = []
    for h in range(GLA_HEADS):
        att = jnp.where(mask, _dot_nt(q_dec[:, dks[h]], k_inv[:, dks[h]]), 0.0).astype(bf16)
        o_intra.append(jnp.dot(att, v_ref[0, :, dvs[h]], preferred_element_type=f32))
        yield
    states = [state_ref[h] for h in range(GLA_HEADS)]
    for ci in (reversed(chunks) if reverse else chunks):
        rs = slice(ci * c, (ci + 1) * c)
        for h in range(GLA_HEADS):
            s = states[h]
            o = o_intra[h][rs] + jnp.dot(q_dec[rs, dks[h]], s.astype(bf16), preferred_element_type=f32)
            o_ref[0, rs, dvs[h]] = o.astype(o_ref.dtype)
            dec = dec_cols[ci][dks[h], :]
            states[h] = (s * jnp.concatenate([dec] * (dv // LANES), axis=1)
                         + _dot_tn(k_end[rs, dks[h]], v_ref[0, rs, dvs[h]]))
            yield
    for h in range(GLA_HEADS):
        state_ref[h] = states[h]


def _gla_kernel(qf_ref, kf_ref, vf_ref, smf_ref, qb_ref, kb_ref, vb_ref, smb_ref,
                w2f_ref, w2b_ref, b2f_ref, b2b_ref, of_ref, ob_ref, state_ref):
    @pl.when(pl.program_id(1) == 0)
    def _():
        state_ref[...] = jnp.zeros_like(state_ref)

    _interleave(_gla_tile(qf_ref, kf_ref, vf_ref, smf_ref, w2f_ref, b2f_ref, of_ref, state_ref.at[0], reverse=False),
                _gla_tile(qb_ref, kb_ref, vb_ref, smb_ref, w2b_ref, b2b_ref, ob_ref, state_ref.at[1], reverse=True))


def _gla(qk, v, small, w2ps, b2s):
    bsz, t, kw2 = qk.shape
    kw = kw2 // 2
    vw = v.shape[-1]
    nt = t // TILE
    const = lambda b, s: (0, 0)
    in_specs, args = [], []
    for reverse in (False, True):
        tile = _scan_tile(nt, reverse)
        tok = lambda cb, tile=tile: (lambda b, s: (b, tile(s), cb))
        in_specs += [pl.BlockSpec((1, TILE, kw), tok(0)),
                     pl.BlockSpec((1, TILE, kw), tok(1)),
                     pl.BlockSpec((1, TILE, vw), tok(0)),
                     pl.BlockSpec((1, TILE, LANES), tok(0))]
        args += [qk, qk, v, small]
    in_specs += [pl.BlockSpec(a.shape, const) for a in (*w2ps, *b2s)]
    args += [*w2ps, *b2s]
    out_specs = [pl.BlockSpec((1, TILE, vw), lambda b, s, tile=_scan_tile(nt, rev): (b, tile(s), 0))
                 for rev in (False, True)]
    return pl.pallas_call(
        _gla_kernel,
        grid=(bsz, nt),
        in_specs=in_specs,
        out_specs=out_specs,
        out_shape=[jax.ShapeDtypeStruct((bsz, t, vw), bf16)] * 2,
        scratch_shapes=[pltpu.VMEM((2, GLA_HEADS, kw // GLA_HEADS, vw // GLA_HEADS), f32)],
        compiler_params=_cparams("parallel", "arbitrary"),
        name="gla",
    )(*args)


def _outmlp_kernel(*refs, n_stream, ff_chunk, final):
    (yf_ref, yb_ref, xs_ref, z_ref, sg_ref, dexp_ref, of_ref, ob_ref, r_ref, gg_ref,
     mod_ref, wo_ref, g2_ref, w1_ref, w2_ref) = refs[n_stream:n_stream + 15]
    o_ref = refs[-1]
    m = mod_ref[0]

    sw = yf_ref.shape[-1]
    gw = sw // SSD_GROUPS
    y = (yf_ref[0] + yb_ref[0]).astype(f32) + dexp_ref[...] * xs_ref[0].astype(f32)
    y = y * _silu(z_ref[0]).astype(f32)
    mix = None
    for g in range(SSD_GROUPS):
        yg = y[:, g * gw:(g + 1) * gw]
        ms = jnp.mean(yg * yg, axis=-1, keepdims=True)
        part = (yg * lax.rsqrt(ms + EPS) * sg_ref[:, g * gw:(g + 1) * gw]).astype(bf16)
        term = jnp.dot(part, wo_ref[g * gw:(g + 1) * gw, :], preferred_element_type=f32)
        mix = term if mix is None else mix + term
    dv = gg_ref.shape[-1]
    o = (of_ref[0] + ob_ref[0]).astype(f32)
    gate = _silu(r_ref[0]).astype(f32)
    for hh in range(GLA_HEADS):
        oh = o[:, hh * dv:(hh + 1) * dv]
        ms = jnp.mean(oh * oh, axis=-1, keepdims=True)
        part = (oh * lax.rsqrt(ms + EPS) * gg_ref[...] * gate[:, hh * dv:(hh + 1) * dv]).astype(bf16)
        mix = mix + jnp.dot(part, wo_ref[sw + hh * dv:sw + (hh + 1) * dv, :], preferred_element_type=f32)
    h1 = _read_stream(refs[:n_stream]) + m[2:3] * mix
    ms = jnp.mean(h1 * h1, axis=-1, keepdims=True)
    u2 = (h1 * lax.rsqrt(ms + EPS) * (g2_ref[...] * (1.0 + m[4:5])) + m[3:4]).astype(bf16)
    acc = jnp.zeros_like(h1)
    for c0 in range(0, w1_ref.shape[-1], ff_chunk):
        hid = jnp.maximum(jnp.dot(u2, w1_ref[:, c0:c0 + ff_chunk], preferred_element_type=f32), 0.0)
        acc = acc + jnp.dot((hid * hid).astype(bf16), w2_ref[c0:c0 + ff_chunk, :], preferred_element_type=f32)
    h2 = h1 + m[5:6] * acc
    if final:
        fg_ref = refs[n_stream + 15]
        ms2 = jnp.mean(h2 * h2, axis=-1, keepdims=True)
        h2 = h2 * lax.rsqrt(ms2 + EPS) * fg_ref[...]
    o_ref[0] = h2


def _outmlp(h, ssd_args, gla_args, mod, wo, g2, w1, w2, ctx_row, final_gain=None):
    yf, yb, xbc, z, sg, dexp = ssd_args
    of, ob, r, gg = gla_args
    bsz, t, sw = yf.shape
    vw = of.shape[-1]
    d = wo.shape[-1]
    nt = t // TILE
    final = final_gain is not None
    skip = 1 if final else 0
    assert not (final and isinstance(h, tuple))
    tok = lambda b, j: (b, j + skip, 0)
    const = lambda b, j: (0, 0)
    resident = lambda a: pl.BlockSpec(a.shape, const, pipeline_mode=pl.Buffered(1))
    if isinstance(h, tuple):
        specs, args = _stream_specs(h, d)
    else:
        specs, args = [pl.BlockSpec((1, TILE, d), tok)], [h]
    n_stream = len(args)
    in_specs = specs + [pl.BlockSpec((1, TILE, sw), tok)] * 4 + [pl.BlockSpec((1, sw), const)] * 2
    in_specs += [pl.BlockSpec((1, TILE, vw), tok)] * 3 + [pl.BlockSpec(gg.shape, const)]
    in_specs += [pl.BlockSpec((1, 6, d), lambda b, j: (jnp.where(j + skip == 0, ctx_row, b), 0, 0)),
                 resident(wo), pl.BlockSpec((1, d), const), resident(w1), resident(w2)]
    args = args + [yf, yb, xbc, z, sg, dexp, of, ob, r, gg, mod, wo, g2, w1, w2]
    if final:
        in_specs.append(pl.BlockSpec((1, d), const))
        args.append(final_gain)
    return pl.pallas_call(
        functools.partial(_outmlp_kernel, n_stream=n_stream, ff_chunk=1024, final=final),
        grid=(bsz, nt - skip),
        in_specs=in_specs,
        out_specs=pl.BlockSpec((1, TILE, d), lambda b, j: (b, j, 0)),
        out_shape=jax.ShapeDtypeStruct((bsz, t - skip * TILE, d), f32),
        compiler_params=_cparams("parallel", "arbitrary"),
        name="outmlp_final" if final else "outmlp",
    )(*args)


def kernel(x, c, ctx, c_ctx, w_ada, b_ada, norm1_g, w_in, conv_w, conv_b, dt_bias, a_log, d_skip,
           ssd_norm_g, gla_w2, gla_b2, gla_norm_g, w_out, norm2_g, w_ff1, w_ff2, final_norm_g):
    bsz, n_lat, d = x.shape
    ctx_len = ctx.shape[1]
    t = ctx_len + n_lat
    depth = w_in.shape[0]
    ssd_w = ssd_norm_g.shape[-1]
    ssd_heads = dt_bias.shape[-1]
    gla_kw = gla_w2.shape[-1]
    gla_dv = gla_norm_g.shape[-1]
    gla_vw = GLA_HEADS * gla_dv
    bc_w = SSD_GROUPS * SSD_STATE
    rank = GLA_GATE_RANK
    assert ctx_len == TILE and n_lat % TILE == 0 and TILE % GRID_W == 0 and n_lat > TILE
    assert ssd_w == ssd_heads * SSD_HEAD_DIM and 2 * ssd_heads + 2 * rank <= LANES
    assert w_in.shape[-1] == 2 * ssd_w + 2 * bc_w + 2 * ssd_heads + 2 * gla_kw + 2 * gla_vw + 2 * rank

    n_rows = -(-(bsz + 1) // SUBLANES) * SUBLANES
    cc = jnp.zeros((n_rows, d), f32).at[:bsz].set(c).at[bsz].set(c_ctx)
    mods = _ada(cc, w_ada, b_ada).reshape(depth, n_rows, 6, d)

    o_dt = 2 * ssd_w + 2 * bc_w
    o_q = o_dt + 2 * ssd_heads
    o_gate = o_q + 2 * gla_kw + 2 * gla_vw
    widths = (ssd_w, ssd_w + 2 * bc_w, 2 * gla_kw, gla_vw, gla_vw, LANES)
    w_in_t = jnp.swapaxes(w_in, 1, 2)
    segments = ((0, o_dt), (o_q, o_gate), (o_dt, o_q), (o_gate, o_gate + 2 * rank))

    def expander(lane_off):
        lanes = jnp.arange(LANES)[:, None]
        heads = jnp.arange(ssd_w)[None, :] // SSD_HEAD_DIM
        e = (lanes == heads + lane_off).astype(bf16)
        return jnp.concatenate([e, e], axis=0)

    def lane_row(vals, off):
        return jnp.zeros((LANES,), f32).at[off:off + vals.shape[0]].set(vals)

    h = (ctx, x)
    for l in range(depth):
        z, xbc, qk, v, r, small = _inproj(h, mods[l], norm1_g[l][None], w_in_t, l, segments,
                                          conv_w[l].reshape(CONV_K * CONV_K, -1), conv_b[l][None],
                                          widths, bsz, bsz, t, GRID_W)

        pvecs, ees = [], []
        for dirn in range(2):
            off = dirn * ssd_heads
            pvec = jnp.zeros((SUBLANES, LANES), f32)
            pvec = pvec.at[0].set(lane_row(dt_bias[l, dirn], off)).at[1].set(lane_row(a_log[l, dirn], off))
            pvecs.append(pvec.at[2].set(lane_row(jnp.ones((ssd_heads,), f32), off)))
            ees.append(expander(off))
        y_f, y_b = _ssd(xbc, small, pvecs, ees, ssd_w, ssd_heads)

        w2ps = []
        for dirn in range(2):
            off = 2 * ssd_heads + dirn * rank
            w2ps.append(jnp.zeros((LANES, gla_kw), f32).at[off:off + rank].set(gla_w2[l, dirn]).astype(bf16))
        o_f, o_b = _gla(qk, v, small, w2ps, [gla_b2[l, 0][None], gla_b2[l, 1][None]])

        last = l == depth - 1
        dexp = jnp.repeat(d_skip[l, 0] + d_skip[l, 1], SSD_HEAD_DIM)[None]
        h = _outmlp(h, (y_f, y_b, xbc, z, ssd_norm_g[l][None], dexp), (o_f, o_b, r, gla_norm_g[l][None]),
                    mods[l], w_out[l].astype(bf16), norm2_g[l][None],
                    w_ff1[l].astype(bf16), w_ff2[l].astype(bf16), bsz,
                    final_norm_g[None] if last else None)
    return h
```

```python
import functools

import jax
import jax.numpy as jnp
from jax import lax
from jax.experimental import pallas as pl
from jax.experimental.pallas import tpu as pltpu

f32 = jnp.float32
bf16 = jnp.bfloat16

GRID_W = 64
SSD_HEAD_DIM = 64
SSD_GROUPS = 2
SSD_STATE = 128
SSD_CHUNK = 128
CONV_K = 3
GLA_HEADS = 4
GLA_GATE_RANK = 16
GLA_GATE_NORM = 16.0
GLA_CHUNK = 64
EPS = 1e-6

TILE = 256
LANES = 128
SUBLANES = 8
VMEM_LIMIT = 56 * 1024 * 1024
LOG2E = 1.4426950408889634


def _cparams(*sem):
    return pltpu.CompilerParams(dimension_semantics=sem, vmem_limit_bytes=VMEM_LIMIT)


def _softplus(x):
    return jnp.maximum(x, 0.0) + jnp.log(1.0 + jnp.exp2(-LOG2E * jnp.abs(x)))


def _silu(x):
    if x.dtype == bf16:
        return x / (1.0 + jnp.exp(-x))
    return x / (1.0 + jnp.exp2(-LOG2E * x))


def _split_dot(lhs_bf16, x):
    hi = x.astype(bf16)
    lo = (x - hi.astype(f32)).astype(bf16)
    return jnp.dot(jnp.concatenate([lhs_bf16, lhs_bf16], axis=1), jnp.concatenate([hi, lo], axis=0),
                   preferred_element_type=f32)


def _dot_nt(a, b):
    return lax.dot_general(a, b, (((1,), (1,)), ((), ())), preferred_element_type=f32)


def _dot_tn(a, b):
    return lax.dot_general(a, b, (((0,), (0,)), ((), ())), preferred_element_type=f32)


def _chunk_masks(chunk, reverse):
    ii = lax.broadcasted_iota(jnp.int32, (TILE, TILE), 0)
    jj = lax.broadcasted_iota(jnp.int32, (TILE, TILE), 1)
    same = (ii // chunk) == (jj // chunk)
    return same & ((jj >= ii) if reverse else (jj <= ii))


def _chunk_ends(cum, chunk, reverse):
    n = TILE // chunk
    return [cum[ci * chunk:ci * chunk + 1] if reverse else cum[(ci + 1) * chunk - 1:(ci + 1) * chunk]
            for ci in range(n)]


def _interleave(*streams):
    live = list(streams)
    while live:
        for s in list(live):
            if next(s, StopIteration) is StopIteration:
                live.remove(s)


def _ada_kernel(cc_ref, w_ref, b_ref, o_ref):
    s = _silu(cc_ref[...]).astype(bf16)
    o_ref[0] = jnp.dot(s, w_ref[0].astype(bf16), preferred_element_type=f32) + b_ref[0]


def _ada(cc, w_ada, b_ada):
    depth, d, n = w_ada.shape
    tn = n // 4
    rows = cc.shape[0]
    return pl.pallas_call(
        _ada_kernel,
        grid=(depth, n // tn),
        in_specs=[pl.BlockSpec((rows, d), lambda l, j: (0, 0)),
                  pl.BlockSpec((1, d, tn), lambda l, j: (l, 0, j)),
                  pl.BlockSpec((1, 1, tn), lambda l, j: (l, 0, j))],
        out_specs=pl.BlockSpec((1, rows, tn), lambda l, j: (l, 0, j)),
        out_shape=jax.ShapeDtypeStruct((depth, rows, n), f32),
        compiler_params=_cparams("arbitrary", "arbitrary"),
        name="ada",
    )(cc, w_ada, b_ada.reshape(depth, 1, n))


def _stream_specs(h, d):
    if isinstance(h, tuple):
        ctx, x = h
        return ([pl.BlockSpec((1, TILE, d), lambda b, j: (b, 0, 0)),
                 pl.BlockSpec((1, TILE, d), lambda b, j: (b, jnp.maximum(j - 1, 0), 0))], [ctx, x])
    return [pl.BlockSpec((1, TILE, d), lambda b, j: (b, j, 0))], [h]


def _read_stream(refs):
    if len(refs) == 2:
        return jnp.where(pl.program_id(1) == 0, refs[0][0], refs[1][0])
    return refs[0][0]


def _inproj_kernel(*refs, n_stream, cols, nt, segments):
    up_ref, dn_ref, mod_ref, g_ref, win_ref, cw_ref, cb_ref = refs[n_stream:n_stream + 7]
    z_ref, xbc_ref, qk_ref, v_ref, r_ref, sm_ref, w_ref = refs[n_stream + 7:]
    j = pl.program_id(1)

    @pl.when((pl.program_id(0) == 0) & (j == 0))
    def _():
        dst = 0
        for lo, hi in segments:
            w_ref[dst:dst + hi - lo, :] = win_ref[0, lo:hi, :].astype(bf16)
            dst += hi - lo
        w_ref[dst:, :] = jnp.zeros((w_ref.shape[0] - dst, w_ref.shape[1]), bf16)
    m = mod_ref[0]
    gain = g_ref[...] * (1.0 + m[1:2])

    def normed(x):
        ms = jnp.mean(x * x, axis=-1, keepdims=True)
        return (x * lax.rsqrt(ms + EPS) * gain + m[0:1]).astype(bf16)

    u = normed(_read_stream(refs[:n_stream]))
    u_ext = jnp.concatenate([normed(up_ref[0]), u, normed(dn_ref[0])], axis=0)

    zw = z_ref.shape[-1]
    cw = xbc_ref.shape[-1]
    xe = _dot_nt(u_ext, w_ref[zw:zw + cw, :])

    is_ctx = j == 0
    lat = jnp.where(is_ctx, 0.0, 1.0)
    top = jnp.where(j <= 1, 0.0, 1.0)
    bot = jnp.where(is_ctx | (j == nt - 1), 0.0, 1.0)
    w = cw_ref[...]
    rows_per_tile = TILE // cols
    xe = xe.astype(bf16)
    parts = []
    for kj in range(CONV_K):
        blocks = []
        for lr in range(rows_per_tile):
            f_up = top if lr == 0 else lat
            f_dn = bot if lr == rows_per_tile - 1 else lat
            blocks.append(xe[lr * cols:(lr + 1) * cols] * (w[kj:kj + 1] * f_up).astype(bf16)
                          + xe[(lr + 1) * cols:(lr + 2) * cols] * w[CONV_K + kj:CONV_K + kj + 1].astype(bf16)
                          + xe[(lr + 2) * cols:(lr + 3) * cols] * (w[2 * CONV_K + kj:2 * CONV_K + kj + 1] * f_dn).astype(bf16))
        parts.append(jnp.concatenate(blocks, axis=0).astype(f32))

    sub = lax.broadcasted_iota(jnp.int32, (SUBLANES, cw), 0)
    rl = pltpu.roll(parts[0], 1, 0)
    rr = pltpu.roll(parts[2], TILE - 1, 0)
    out_blocks = []
    for t0 in range(0, TILE, SUBLANES):
        lb = rl[t0:t0 + SUBLANES]
        rb = rr[t0:t0 + SUBLANES]
        if t0 % cols == 0:
            keep = (sub >= 1) if t0 == 0 else ((sub >= 1) | is_ctx)
            lb = jnp.where(keep, lb, 0.0)
        if (t0 + SUBLANES) % cols == 0:
            keep = (sub <= SUBLANES - 2) if t0 + SUBLANES == TILE else ((sub <= SUBLANES - 2) | is_ctx)
            rb = jnp.where(keep, rb, 0.0)
        out_blocks.append(lb + rb)
    conv = jnp.concatenate(out_blocks, axis=0) + parts[1] + cb_ref[...]
    xbc_ref[0] = _silu(conv).astype(xbc_ref.dtype)

    col = 0
    for ref in (z_ref, xbc_ref, qk_ref, v_ref, r_ref, sm_ref):
        width = ref.shape[-1]
        if ref is not xbc_ref:
            ref[0] = _dot_nt(u, w_ref[col:col + width, :]).astype(ref.dtype)
        col += width


def _inproj(h, mod, g, w_in, layer, segments, cw9, cbias, widths, ctx_row, bsz, t, cols):
    d = w_in.shape[2]
    nt = t // TILE
    rpt = TILE // cols
    tok = lambda b, j: (b, j, 0)
    specs, args = _stream_specs(h, d)
    if isinstance(h, tuple):
        lat, first, last = h[1], 0, (t - TILE) // cols - 1
    else:
        lat, first, last = h, rpt, t // cols - 1
    up = lambda b, j: (b, jnp.clip(first + (j - 1) * rpt - 1, first, last), 0)
    dn = lambda b, j: (b, jnp.clip(first + j * rpt, first, last), 0)
    const = lambda b, j: (0, 0)
    out_shape = [jax.ShapeDtypeStruct((bsz, t, wd), bf16) for wd in widths[:-1]]
    out_shape.append(jax.ShapeDtypeStruct((bsz, t, widths[-1]), f32))
    return pl.pallas_call(
        functools.partial(_inproj_kernel, n_stream=len(args), cols=cols, nt=nt, segments=segments),
        grid=(bsz, nt),
        in_specs=specs + [pl.BlockSpec((1, cols, d), up), pl.BlockSpec((1, cols, d), dn),
                          pl.BlockSpec((1, 6, d), lambda b, j: (jnp.where(j == 0, ctx_row, b), 0, 0)),
                          pl.BlockSpec((1, d), const),
                          pl.BlockSpec((1,) + w_in.shape[1:], lambda b, j: (layer, 0, 0),
                                       pipeline_mode=pl.Buffered(1)),
                          pl.BlockSpec(cw9.shape, const),
                          pl.BlockSpec(cbias.shape, const)],
        out_specs=[pl.BlockSpec((1, TILE, wd), tok) for wd in widths],
        out_shape=out_shape,
        scratch_shapes=[pltpu.VMEM((sum(widths), d), bf16)],
        compiler_params=_cparams("arbitrary", "arbitrary"),
        name="inproj",
    )(*args, lat, lat, mod, g, w_in, cw9, cbias)


def _ssd_tile(x_ref, b_ref, c_ref, sm_ref, pv_ref, ee_ref, o_ref, state_ref, *, reverse, lane_off):
    q = SSD_CHUNK
    n = SSD_STATE
    gw = state_ref.shape[-1]
    pairs_per_group = gw // LANES
    hd = SSD_HEAD_DIM

    tri = _chunk_masks(q, reverse).astype(bf16)
    ci_ = lax.broadcasted_iota(jnp.int32, (q, q), 0)
    cj_ = lax.broadcasted_iota(jnp.int32, (q, q), 1)
    mask = (cj_ >= ci_) if reverse else (cj_ <= ci_)
    lo_half = lax.broadcasted_iota(jnp.int32, (q, LANES), 1) < hd
    bias = pv_ref[0:1, :]
    a_coef = -jnp.exp(pv_ref[1:2, :]) * pv_ref[2:3, :] * LOG2E
    ee = ee_ref[...]

    def expand(v):
        hi = v.astype(bf16)
        lo = (v - hi.astype(f32)).astype(bf16)
        return jnp.dot(jnp.concatenate([hi, lo], axis=1), ee, preferred_element_type=f32)

    dt = _softplus(sm_ref[0] + bias)
    ac = _split_dot(tri, dt * a_coef)
    ends = _chunk_ends(ac, q, reverse)
    ac_end = jnp.concatenate([jnp.broadcast_to(e, (q, LANES)) for e in ends], axis=0)
    wexp = expand(dt * jnp.exp2(ac_end - ac)).astype(bf16)
    eexp = expand(jnp.exp2(ac))
    src_t = (ac - jnp.log2(dt)).T
    yield

    chunks = range(TILE // q)
    for ci in (reversed(chunks) if reverse else chunks):
        rs = slice(ci * q, (ci + 1) * q)
        end_row = ci * q if reverse else (ci + 1) * q - 1
        xs = x_ref[0, rs, :]
        xw = xs * wexp[rs]
        for g in range(SSD_GROUPS):
            gs = slice(g * gw, (g + 1) * gw)
            bg = b_ref[0, rs, g * n:(g + 1) * n]
            cg = c_ref[0, rs, g * n:(g + 1) * n]
            cbm16 = _dot_nt(cg, bg).astype(bf16)
            s_g = state_ref[g]
            y_off = jnp.dot(cg, s_g.astype(bf16), preferred_element_type=f32) * eexp[rs, gs]
            ys = []
            for pp in range(pairs_per_group):
                p = g * pairs_per_group + pp
                pieces = []
                for h in (2 * p, 2 * p + 1):
                    li = lane_off + h
                    colb = jnp.broadcast_to(ac[rs, li:li + 1], (q, q))
                    dec = jnp.exp2(colb - src_t[li:li + 1, rs]).astype(bf16)
                    pieces.append(cbm16 * jnp.where(mask, dec, jnp.zeros_like(dec)))
                lhs = jnp.concatenate(pieces, axis=1)
                xp = xs[:, p * LANES:(p + 1) * LANES]
                zero = jnp.zeros_like(xp)
                rhs = jnp.concatenate([jnp.where(lo_half, xp, zero), jnp.where(lo_half, zero, xp)], axis=0)
                ys.append(jnp.dot(lhs, rhs, preferred_element_type=f32))
                yield
            state_ref[g] = s_g * eexp[end_row:end_row + 1, gs] + _dot_tn(bg, xw[:, gs])
            o_ref[0, rs, gs] = (jnp.concatenate(ys, axis=1) + y_off).astype(o_ref.dtype)
            yield


def _ssd_kernel(xf_ref, bf_ref, cf_ref, smf_ref, xb_ref, bb_ref, cb_ref, smb_ref,
                pvf_ref, pvb_ref, eef_ref, eeb_ref, of_ref, ob_ref, state_ref, *, heads):
    @pl.when(pl.program_id(1) == 0)
    def _():
        state_ref[...] = jnp.zeros_like(state_ref)

    _interleave(_ssd_tile(xf_ref, bf_ref, cf_ref, smf_ref, pvf_ref, eef_ref, of_ref, state_ref.at[0],
                          reverse=False, lane_off=0),
                _ssd_tile(xb_ref, bb_ref, cb_ref, smb_ref, pvb_ref, eeb_ref, ob_ref, state_ref.at[1],
                          reverse=True, lane_off=heads))


def _scan_tile(nt, reverse):
    if reverse:
        return lambda s: jnp.where(s == 0, 0, nt - s)
    return lambda s: s


def _ssd(xbc, small, pvecs, ees, width, heads):
    bsz, t, _ = xbc.shape
    nt = t // TILE
    n = SSD_STATE
    gw = width // SSD_GROUPS
    bcb = width // (SSD_GROUPS * n)
    const = lambda b, s: (0, 0)
    in_specs, args = [], []
    for reverse in (False, True):
        tile = _scan_tile(nt, reverse)
        tok = lambda cb, tile=tile: (lambda b, s: (b, tile(s), cb))
        in_specs += [pl.BlockSpec((1, TILE, width), tok(0)),
                     pl.BlockSpec((1, TILE, SSD_GROUPS * n), tok(bcb)),
                     pl.BlockSpec((1, TILE, SSD_GROUPS * n), tok(bcb + 1)),
                     pl.BlockSpec((1, TILE, LANES), tok(0))]
        args += [xbc, xbc, xbc, small]
    in_specs += [pl.BlockSpec(a.shape, const) for a in (*pvecs, *ees)]
    args += [*pvecs, *ees]
    out_specs = [pl.BlockSpec((1, TILE, width), lambda b, s, tile=_scan_tile(nt, rev): (b, tile(s), 0))
                 for rev in (False, True)]
    return pl.pallas_call(
        functools.partial(_ssd_kernel, heads=heads),
        grid=(bsz, nt),
        in_specs=in_specs,
        out_specs=out_specs,
        out_shape=[jax.ShapeDtypeStruct((bsz, t, width), bf16)] * 2,
        scratch_shapes=[pltpu.VMEM((2, SSD_GROUPS, n, gw), f32)],
        compiler_params=_cparams("parallel", "arbitrary"),
        name="ssd",
    )(*args)


def _gla_tile(q_ref, k_ref, v_ref, sm_ref, w2_ref, b2_ref, o_ref, state_ref, *, reverse):
    c = GLA_CHUNK
    dk = state_ref.shape[1]
    dv = state_ref.shape[2]
    kw = GLA_HEADS * dk
    scale = dk ** -0.5

    mask = _chunk_masks(c, reverse)
    tri = mask.astype(bf16)

    gate = jnp.dot(sm_ref[0].astype(bf16), w2_ref[...], preferred_element_type=f32) + b2_ref[...]
    log_a = -_softplus(-gate) * (LOG2E / GLA_GATE_NORM)
    bc = _split_dot(tri, log_a)
    yield
    ends = _chunk_ends(bc, c, reverse)
    b_end = jnp.concatenate([jnp.broadcast_to(e, (c, kw)) for e in ends], axis=0)
    q_dec = q_ref[0] * (jnp.exp2(bc) * scale).astype(bf16)
    k16 = k_ref[0]
    k_inv = k16 * jnp.exp2(-bc).astype(bf16)
    k_end = k16 * jnp.exp2(b_end - bc).astype(bf16)
    dec_cols = [jnp.broadcast_to(jnp.exp2(e), (LANES, kw)).T for e in ends]
    yield

    chunks = range(TILE // c)
    dks = [slice(h * dk, (h + 1) * dk) for h in range(GLA_HEADS)]
    dvs = [slice(h * dv, (h + 1) * dv) for h in range(GLA_HEADS)]
    o_intra = []
    for h in range(GLA_HEADS):
        att = jnp.where(mask, _dot_nt(q_dec[:, dks[h]], k_inv[:, dks[h]]), 0.0).astype(bf16)
        o_intra.append(jnp.dot(att, v_ref[0, :, dvs[h]], preferred_element_type=f32))
        yield
    states = [state_ref[h] for h in range(GLA_HEADS)]
    for ci in (reversed(chunks) if reverse else chunks):
        rs = slice(ci * c, (ci + 1) * c)
        for h in range(GLA_HEADS):
            s = states[h]
            o = o_intra[h][rs] + jnp.dot(q_dec[rs, dks[h]], s.astype(bf16), preferred_element_type=f32)
            o_ref[0, rs, dvs[h]] = o.astype(o_ref.dtype)
            dec = dec_cols[ci][dks[h], :]
            states[h] = (s * jnp.concatenate([dec] * (dv // LANES), axis=1)
                         + _dot_tn(k_end[rs, dks[h]], v_ref[0, rs, dvs[h]]))
            yield
    for h in range(GLA_HEADS):
        state_ref[h] = states[h]


def _gla_kernel(qf_ref, kf_ref, vf_ref, smf_ref, qb_ref, kb_ref, vb_ref, smb_ref,
                w2f_ref, w2b_ref, b2f_ref, b2b_ref, of_ref, ob_ref, state_ref):
    @pl.when(pl.program_id(1) == 0)
    def _():
        state_ref[...] = jnp.zeros_like(state_ref)

    _interleave(_gla_tile(qf_ref, kf_ref, vf_ref, smf_ref, w2f_ref, b2f_ref, of_ref, state_ref.at[0], reverse=False),
                _gla_tile(qb_ref, kb_ref, vb_ref, smb_ref, w2b_ref, b2b_ref, ob_ref, state_ref.at[1], reverse=True))


def _gla(qk, v, small, w2ps, b2s):
    bsz, t, kw2 = qk.shape
    kw = kw2 // 2
    vw = v.shape[-1]
    nt = t // TILE
    const = lambda b, s: (0, 0)
    in_specs, args = [], []
    for reverse in (False, True):
        tile = _scan_tile(nt, reverse)
        tok = lambda cb, tile=tile: (lambda b, s: (b, tile(s), cb))
        in_specs += [pl.BlockSpec((1, TILE, kw), tok(0)),
                     pl.BlockSpec((1, TILE, kw), tok(1)),
                     pl.BlockSpec((1, TILE, vw), tok(0)),
                     pl.BlockSpec((1, TILE, LANES), tok(0))]
        args += [qk, qk, v, small]
    in_specs += [pl.BlockSpec(a.shape, const) for a in (*w2ps, *b2s)]
    args += [*w2ps, *b2s]
    out_specs = [pl.BlockSpec((1, TILE, vw), lambda b, s, tile=_scan_tile(nt, rev): (b, tile(s), 0))
                 for rev in (False, True)]
    return pl.pallas_call(
        _gla_kernel,
        grid=(bsz, nt),
        in_specs=in_specs,
        out_specs=out_specs,
        out_shape=[jax.ShapeDtypeStruct((bsz, t, vw), bf16)] * 2,
        scratch_shapes=[pltpu.VMEM((2, GLA_HEADS, kw // GLA_HEADS, vw // GLA_HEADS), f32)],
        compiler_params=_cparams("parallel", "arbitrary"),
        name="gla",
    )(*args)


def _outmlp_kernel(*refs, n_stream, ff_chunk, final):
    (yf_ref, yb_ref, xs_ref, z_ref, sg_ref, dexp_ref, of_ref, ob_ref, r_ref, gg_ref,
     mod_ref, wo_ref, g2_ref, w1_ref, w2_ref) = refs[n_stream:n_stream + 15]
    o_ref = refs[-1]
    m = mod_ref[0]

    sw = yf_ref.shape[-1]
    gw = sw // SSD_GROUPS
    y = (yf_ref[0] + yb_ref[0]).astype(f32) + dexp_ref[...] * xs_ref[0].astype(f32)
    y = y * _silu(z_ref[0]).astype(f32)
    mix = None
    for g in range(SSD_GROUPS):
        yg = y[:, g * gw:(g + 1) * gw]
        ms = jnp.mean(yg * yg, axis=-1, keepdims=True)
        part = (yg * lax.rsqrt(ms + EPS) * sg_ref[:, g * gw:(g + 1) * gw]).astype(bf16)
        term = jnp.dot(part, wo_ref[g * gw:(g + 1) * gw, :], preferred_element_type=f32)
        mix = term if mix is None else mix + term
    dv = gg_ref.shape[-1]
    o = (of_ref[0] + ob_ref[0]).astype(f32)
    gate = _silu(r_ref[0]).astype(f32)
    for hh in range(GLA_HEADS):
        oh = o[:, hh * dv:(hh + 1) * dv]
        ms = jnp.mean(oh * oh, axis=-1, keepdims=True)
        part = (oh * lax.rsqrt(ms + EPS) * gg_ref[...] * gate[:, hh * dv:(hh + 1) * dv]).astype(bf16)
        mix = mix + jnp.dot(part, wo_ref[sw + hh * dv:sw + (hh + 1) * dv, :], preferred_element_type=f32)
    h1 = _read_stream(refs[:n_stream]) + m[2:3] * mix
    ms = jnp.mean(h1 * h1, axis=-1, keepdims=True)
    u2 = (h1 * lax.rsqrt(ms + EPS) * (g2_ref[...] * (1.0 + m[4:5])) + m[3:4]).astype(bf16)
    acc = jnp.zeros_like(h1)
    for c0 in range(0, w1_ref.shape[-1], ff_chunk):
        hid = jnp.maximum(jnp.dot(u2, w1_ref[:, c0:c0 + ff_chunk], preferred_element_type=f32), 0.0)
        acc = acc + jnp.dot((hid * hid).astype(bf16), w2_ref[c0:c0 + ff_chunk, :], preferred_element_type=f32)
    h2 = h1 + m[5:6] * acc
    if final:
        fg_ref = refs[n_stream + 15]
        ms2 = jnp.mean(h2 * h2, axis=-1, keepdims=True)
        h2 = h2 * lax.rsqrt(ms2 + EPS) * fg_ref[...]
    o_ref[0] = h2


def _outmlp(h, ssd_args, gla_args, mod, wo, g2, w1, w2, ctx_row, final_gain=None):
    yf, yb, xbc, z, sg, dexp = ssd_args
    of, ob, r, gg = gla_args
    bsz, t, sw = yf.shape
    vw = of.shape[-1]
    d = wo.shape[-1]
    nt = t // TILE
    final = final_gain is not None
    skip = 1 if final else 0
    assert not (final and isinstance(h, tuple))
    tok = lambda b, j: (b, j + skip, 0)
    const = lambda b, j: (0, 0)
    resident = lambda a: pl.BlockSpec(a.shape, const, pipeline_mode=pl.Buffered(1))
    if isinstance(h, tuple):
        specs, args = _stream_specs(h, d)
    else:
        specs, args = [pl.BlockSpec((1, TILE, d), tok)], [h]
    n_stream = len(args)
    in_specs = specs + [pl.BlockSpec((1, TILE, sw), tok)] * 4 + [pl.BlockSpec((1, sw), const)] * 2
    in_specs += [pl.BlockSpec((1, TILE, vw), tok)] * 3 + [pl.BlockSpec(gg.shape, const)]
    in_specs += [pl.BlockSpec((1, 6, d), lambda b, j: (jnp.where(j + skip == 0, ctx_row, b), 0, 0)),
                 resident(wo), pl.BlockSpec((1, d), const), resident(w1), resident(w2)]
    args = args + [yf, yb, xbc, z, sg, dexp, of, ob, r, gg, mod, wo, g2, w1, w2]
    if final:
        in_specs.append(pl.BlockSpec((1, d), const))
        args.append(final_gain)
    return pl.pallas_call(
        functools.partial(_outmlp_kernel, n_stream=n_stream, ff_chunk=1024, final=final),
        grid=(bsz, nt - skip),
        in_specs=in_specs,
        out_specs=pl.BlockSpec((1, TILE, d), lambda b, j: (b, j, 0)),
        out_shape=jax.ShapeDtypeStruct((bsz, t - skip * TILE, d), f32),
        compiler_params=_cparams("parallel", "arbitrary"),
        name="outmlp_final" if final else "outmlp",
    )(*args)


def kernel(x, c, ctx, c_ctx, w_ada, b_ada, norm1_g, w_in, conv_w, conv_b, dt_bias, a_log, d_skip,
           ssd_norm_g, gla_w2, gla_b2, gla_norm_g, w_out, norm2_g, w_ff1, w_ff2, final_norm_g):
    bsz, n_lat, d = x.shape
    ctx_len = ctx.shape[1]
    t = ctx_len + n_lat
    depth = w_in.shape[0]
    ssd_w = ssd_norm_g.shape[-1]
    ssd_heads = dt_bias.shape[-1]
    gla_kw = gla_w2.shape[-1]
    gla_dv = gla_norm_g.shape[-1]
    gla_vw = GLA_HEADS * gla_dv
    bc_w = SSD_GROUPS * SSD_STATE
    rank = GLA_GATE_RANK
    assert ctx_len == TILE and n_lat % TILE == 0 and TILE % GRID_W == 0 and n_lat > TILE
    assert ssd_w == ssd_heads * SSD_HEAD_DIM and 2 * ssd_heads + 2 * rank <= LANES
    assert w_in.shape[-1] == 2 * ssd_w + 2 * bc_w + 2 * ssd_heads + 2 * gla_kw + 2 * gla_vw + 2 * rank

    n_rows = -(-(bsz + 1) // SUBLANES) * SUBLANES
    cc = jnp.zeros((n_rows, d), f32).at[:bsz].set(c).at[bsz].set(c_ctx)
    mods = _ada(cc, w_ada, b_ada).reshape(depth, n_rows, 6, d)

    o_dt = 2 * ssd_w + 2 * bc_w
    o_q = o_dt + 2 * ssd_heads
    o_gate = o_q + 2 * gla_kw + 2 * gla_vw
    widths = (ssd_w, ssd_w + 2 * bc_w, 2 * gla_kw, gla_vw, gla_vw, LANES)
    w_in_t = jnp.swapaxes(w_in, 1, 2)
    segments = ((0, o_dt), (o_q, o_gate), (o_dt, o_q), (o_gate, o_gate + 2 * rank))

    def expander(lane_off):
        lanes = jnp.arange(LANES)[:, None]
        heads = jnp.arange(ssd_w)[None, :] // SSD_HEAD_DIM
        e = (lanes == heads + lane_off).astype(bf16)
        return jnp.concatenate([e, e], axis=0)

    def lane_row(vals, off):
        return jnp.zeros((LANES,), f32).at[off:off + vals.shape[0]].set(vals)

    h = (ctx, x)
    for l in range(depth):
        z, xbc, qk, v, r, small = _inproj(h, mods[l], norm1_g[l][None], w_in_t, l, segments,
                                          conv_w[l].reshape(CONV_K * CONV_K, -1), conv_b[l][None],
                                          widths, bsz, bsz, t, GRID_W)

        pvecs, ees = [], []
        for dirn in range(2):
            off = dirn * ssd_heads
            pvec = jnp.zeros((SUBLANES, LANES), f32)
            pvec = pvec.at[0].set(lane_row(dt_bias[l, dirn], off)).at[1].set(lane_row(a_log[l, dirn], off))
            pvecs.append(pvec.at[2].set(lane_row(jnp.ones((ssd_heads,), f32), off)))
            ees.append(expander(off))
        y_f, y_b = _ssd(xbc, small, pvecs, ees, ssd_w, ssd_heads)

        w2ps = []
        for dirn in range(2):
            off = 2 * ssd_heads + dirn * rank
            w2ps.append(jnp.zeros((LANES, gla_kw), f32).at[off:off + rank].set(gla_w2[l, dirn]).astype(bf16))
        o_f, o_b = _gla(qk, v, small, w2ps, [gla_b2[l, 0][None], gla_b2[l, 1][None]])

        last = l == depth - 1
        dexp = jnp.repeat(d_skip[l, 0] + d_skip[l, 1], SSD_HEAD_DIM)[None]
        h = _outmlp(h, (y_f, y_b, xbc, z, ssd_norm_g[l][None], dexp), (o_f, o_b, r, gla_norm_g[l][None]),
                    mods[l], w_out[l].astype(bf16), norm2_g[l][None],
                    w_ff1[l].astype(bf16), w_ff2[l].astype(bf16), bsz,
                    final_norm_g[None] if last else None)
    return h
```

```python
import functools

import jax
import jax.numpy as jnp
from jax import lax
from jax.experimental import pallas as pl
from jax.experimental.pallas import tpu as pltpu

f32 = jnp.float32
bf16 = jnp.bfloat16

GRID_W = 64
SSD_HEAD_DIM = 64
SSD_GROUPS = 2
SSD_STATE = 128
SSD_CHUNK = 128
CONV_K = 3
GLA_HEADS = 4
GLA_GATE_RANK = 16
GLA_GATE_NORM = 16.0
GLA_CHUNK = 64
EPS = 1e-6

TILE = 256
LANES = 128
SUBLANES = 8
VMEM_LIMIT = 56 * 1024 * 1024
LOG2E = 1.4426950408889634


def _cparams(*sem):
    return pltpu.CompilerParams(dimension_semantics=sem, vmem_limit_bytes=VMEM_LIMIT)


def _softplus(x):
    return jnp.maximum(x, 0.0) + jnp.log(1.0 + jnp.exp2(-LOG2E * jnp.abs(x)))


def _silu(x):
    if x.dtype == bf16:
        return x / (1.0 + jnp.exp(-x))
    return x / (1.0 + jnp.exp2(-LOG2E * x))


def _split_dot(lhs_bf16, x):
    hi = x.astype(bf16)
    lo = (x - hi.astype(f32)).astype(bf16)
    return jnp.dot(jnp.concatenate([lhs_bf16, lhs_bf16], axis=1), jnp.concatenate([hi, lo], axis=0),
                   preferred_element_type=f32)


def _dot_nt(a, b):
    return lax.dot_general(a, b, (((1,), (1,)), ((), ())), preferred_element_type=f32)


def _dot_tn(a, b):
    return lax.dot_general(a, b, (((0,), (0,)), ((), ())), preferred_element_type=f32)


def _chunk_masks(chunk, reverse):
    ii = lax.broadcasted_iota(jnp.int32, (TILE, TILE), 0)
    jj = lax.broadcasted_iota(jnp.int32, (TILE, TILE), 1)
    same = (ii // chunk) == (jj // chunk)
    return same & ((jj >= ii) if reverse else (jj <= ii))


def _chunk_ends(cum, chunk, reverse):
    n = TILE // chunk
    return [cum[ci * chunk:ci * chunk + 1] if reverse else cum[(ci + 1) * chunk - 1:(ci + 1) * chunk]
            for ci in range(n)]


def _interleave(*streams):
    live = list(streams)
    while live:
        for s in list(live):
            if next(s, StopIteration) is StopIteration:
                live.remove(s)


def _ada_kernel(cc_ref, w_ref, b_ref, o_ref):
    s = _silu(cc_ref[...]).astype(bf16)
    o_ref[0] = jnp.dot(s, w_ref[0].astype(bf16), preferred_element_type=f32) + b_ref[0]


def _ada(cc, w_ada, b_ada):
    depth, d, n = w_ada.shape
    tn = n // 4
    rows = cc.shape[0]
    return pl.pallas_call(
        _ada_kernel,
        grid=(depth, n // tn),
        in_specs=[pl.BlockSpec((rows, d), lambda l, j: (0, 0)),
                  pl.BlockSpec((1, d, tn), lambda l, j: (l, 0, j)),
                  pl.BlockSpec((1, 1, tn), lambda l, j: (l, 0, j))],
        out_specs=pl.BlockSpec((1, rows, tn), lambda l, j: (l, 0, j)),
        out_shape=jax.ShapeDtypeStruct((depth, rows, n), f32),
        compiler_params=_cparams("arbitrary", "arbitrary"),
        name="ada",
    )(cc, w_ada, b_ada.reshape(depth, 1, n))


def _stream_specs(h, d):
    if isinstance(h, tuple):
        ctx, x = h
        return ([pl.BlockSpec((1, TILE, d), lambda b, j: (b, 0, 0)),
                 pl.BlockSpec((1, TILE, d), lambda b, j: (b, jnp.maximum(j - 1, 0), 0))], [ctx, x])
    return [pl.BlockSpec((1, TILE, d), lambda b, j: (b, j, 0))], [h]


def _read_stream(refs):
    if len(refs) == 2:
        return jnp.where(pl.program_id(1) == 0, refs[0][0], refs[1][0])
    return refs[0][0]


def _inproj_kernel(*refs, n_stream, cols, nt, segments):
    up_ref, dn_ref, mod_ref, g_ref, win_ref, cw_ref, cb_ref = refs[n_stream:n_stream + 7]
    z_ref, xbc_ref, qk_ref, v_ref, r_ref, sm_ref, w_ref = refs[n_stream + 7:]
    j = pl.program_id(1)

    @pl.when((pl.program_id(0) == 0) & (j == 0))
    def _():
        dst = 0
        for lo, hi in segments:
            w_ref[dst:dst + hi - lo, :] = win_ref[0, lo:hi, :].astype(bf16)
            dst += hi - lo
        w_ref[dst:, :] = jnp.zeros((w_ref.shape[0] - dst, w_ref.shape[1]), bf16)
    m = mod_ref[0]
    gain = g_ref[...] * (1.0 + m[1:2])

    def normed(x):
        ms = jnp.mean(x * x, axis=-1, keepdims=True)
        return (x * lax.rsqrt(ms + EPS) * gain + m[0:1]).astype(bf16)

    u = normed(_read_stream(refs[:n_stream]))
    u_ext = jnp.concatenate([normed(up_ref[0]), u, normed(dn_ref[0])], axis=0)

    zw = z_ref.shape[-1]
    cw = xbc_ref.shape[-1]
    xe = _dot_nt(u_ext, w_ref[zw:zw + cw, :])

    is_ctx = j == 0
    lat = jnp.where(is_ctx, 0.0, 1.0)
    top = jnp.where(j <= 1, 0.0, 1.0)
    bot = jnp.where(is_ctx | (j == nt - 1), 0.0, 1.0)
    w = cw_ref[...]
    rows_per_tile = TILE // cols
    xe = xe.astype(bf16)
    parts = []
    for kj in range(CONV_K):
        blocks = []
        for lr in range(rows_per_tile):
            f_up = top if lr == 0 else lat
            f_dn = bot if lr == rows_per_tile - 1 else lat
            blocks.append(xe[lr * cols:(lr + 1) * cols] * (w[kj:kj + 1] * f_up).astype(bf16)
                          + xe[(lr + 1) * cols:(lr + 2) * cols] * w[CONV_K + kj:CONV_K + kj + 1].astype(bf16)
                          + xe[(lr + 2) * cols:(lr + 3) * cols] * (w[2 * CONV_K + kj:2 * CONV_K + kj + 1] * f_dn).astype(bf16))
        parts.append(jnp.concatenate(blocks, axis=0).astype(f32))

    sub = lax.broadcasted_iota(jnp.int32, (SUBLANES, cw), 0)
    rl = pltpu.roll(parts[0], 1, 0)
    rr = pltpu.roll(parts[2], TILE - 1, 0)
    out_blocks = []
    for t0 in range(0, TILE, SUBLANES):
        lb = rl[t0:t0 + SUBLANES]
        rb = rr[t0:t0 + SUBLANES]
        if t0 % cols == 0:
            keep = (sub >= 1) if t0 == 0 else ((sub >= 1) | is_ctx)
            lb = jnp.where(keep, lb, 0.0)
        if (t0 + SUBLANES) % cols == 0:
            keep = (sub <= SUBLANES - 2) if t0 + SUBLANES == TILE else ((sub <= SUBLANES - 2) | is_ctx)
            rb = jnp.where(keep, rb, 0.0)
        out_blocks.append(lb + rb)
    conv = jnp.concatenate(out_blocks, axis=0) + parts[1] + cb_ref[...]
    xbc_ref[0] = _silu(conv).astype(xbc_ref.dtype)

    col = 0
    for ref in (z_ref, xbc_ref, qk_ref, v_ref, r_ref, sm_ref):
        width = ref.shape[-1]
        if ref is not xbc_ref:
            ref[0] = _dot_nt(u, w_ref[col:col + width, :]).astype(ref.dtype)
        col += width


def _inproj(h, mod, g, w_in, layer, segments, cw9, cbias, widths, ctx_row, bsz, t, cols):
    d = w_in.shape[2]
    nt = t // TILE
    rpt = TILE // cols
    tok = lambda b, j: (b, j, 0)
    specs, args = _stream_specs(h, d)
    if isinstance(h, tuple):
        lat, first, last = h[1], 0, (t - TILE) // cols - 1
    else:
        lat, first, last = h, rpt, t // cols - 1
    up = lambda b, j: (b, jnp.clip(first + (j - 1) * rpt - 1, first, last), 0)
    dn = lambda b, j: (b, jnp.clip(first + j * rpt, first, last), 0)
    const = lambda b, j: (0, 0)
    out_shape = [jax.ShapeDtypeStruct((bsz, t, wd), bf16) for wd in widths[:-1]]
    out_shape.append(jax.ShapeDtypeStruct((bsz, t, widths[-1]), f32))
    return pl.pallas_call(
        functools.partial(_inproj_kernel, n_stream=len(args), cols=cols, nt=nt, segments=segments),
        grid=(bsz, nt),
        in_specs=specs + [pl.BlockSpec((1, cols, d), up), pl.BlockSpec((1, cols, d), dn),
                          pl.BlockSpec((1, 6, d), lambda b, j: (jnp.where(j == 0, ctx_row, b), 0, 0)),
                          pl.BlockSpec((1, d), const),
                          pl.BlockSpec((1,) + w_in.shape[1:], lambda b, j: (layer, 0, 0),
                                       pipeline_mode=pl.Buffered(1)),
                          pl.BlockSpec(cw9.shape, const),
                          pl.BlockSpec(cbias.shape, const)],
        out_specs=[pl.BlockSpec((1, TILE, wd), tok) for wd in widths],
        out_shape=out_shape,
        scratch_shapes=[pltpu.VMEM((sum(widths), d), bf16)],
        compiler_params=_cparams("arbitrary", "arbitrary"),
        name="inproj",
    )(*args, lat, lat, mod, g, w_in, cw9, cbias)


def _ssd_tile(x_ref, b_ref, c_ref, sm_ref, pv_ref, ee_ref, o_ref, state_ref, *, reverse, lane_off):
    q = SSD_CHUNK
    n = SSD_STATE
    gw = state_ref.shape[-1]
    pairs_per_group = gw // LANES
    hd = SSD_HEAD_DIM

    tri = _chunk_masks(q, reverse).astype(bf16)
    ci_ = lax.broadcasted_iota(jnp.int32, (q, q), 0)
    cj_ = lax.broadcasted_iota(jnp.int32, (q, q), 1)
    mask = (cj_ >= ci_) if reverse else (cj_ <= ci_)
    lo_half = lax.broadcasted_iota(jnp.int32, (q, LANES), 1) < hd
    bias = pv_ref[0:1, :]
    a_coef = -jnp.exp(pv_ref[1:2, :]) * pv_ref[2:3, :] * LOG2E
    ee = ee_ref[...]

    def expand(v):
        hi = v.astype(bf16)
        lo = (v - hi.astype(f32)).astype(bf16)
        return jnp.dot(jnp.concatenate([hi, lo], axis=1), ee, preferred_element_type=f32)

    dt = _softplus(sm_ref[0] + bias)
    ac = _split_dot(tri, dt * a_coef)
    ends = _chunk_ends(ac, q, reverse)
    ac_end = jnp.concatenate([jnp.broadcast_to(e, (q, LANES)) for e in ends], axis=0)
    wgt = dt * jnp.exp2(ac_end - ac)
    dcy = jnp.exp2(ac)
    src_t = (ac - jnp.log2(dt)).T
    yield

    chunks = range(TILE // q)
    for ci in (reversed(chunks) if reverse else chunks):
        rs = slice(ci * q, (ci + 1) * q)
        end_row = ci * q if reverse else (ci + 1) * q - 1
        xs = x_ref[0, rs, :]
        xw = xs * expand(wgt[rs]).astype(bf16)
        eexp_c = expand(dcy[rs])
        for g in range(SSD_GROUPS):
            gs = slice(g * gw, (g + 1) * gw)
            bg = b_ref[0, rs, g * n:(g + 1) * n]
            cg = c_ref[0, rs, g * n:(g + 1) * n]
            cbm16 = _dot_nt(cg, bg).astype(bf16)
            s_g = state_ref[g]
            y_off = jnp.dot(cg, s_g.astype(bf16), preferred_element_type=f32) * eexp_c[:, gs]
            ys = []
            for pp in range(pairs_per_group):
                p = g * pairs_per_group + pp
                pieces = []
                for h in (2 * p, 2 * p + 1):
                    li = lane_off + h
                    colb = jnp.broadcast_to(ac[rs, li:li + 1], (q, q))
                    dec = jnp.exp2(colb - src_t[li:li + 1, rs]).astype(bf16)
                    pieces.append(cbm16 * jnp.where(mask, dec, jnp.zeros_like(dec)))
                lhs = jnp.concatenate(pieces, axis=1)
                xp = xs[:, p * LANES:(p + 1) * LANES]
                zero = jnp.zeros_like(xp)
                rhs = jnp.concatenate([jnp.where(lo_half, xp, zero), jnp.where(lo_half, zero, xp)], axis=0)
                ys.append(jnp.dot(lhs, rhs, preferred_element_type=f32))
                yield
            state_ref[g] = s_g * eexp_c[end_row - ci * q:end_row - ci * q + 1, gs] + _dot_tn(bg, xw[:, gs])
            o_ref[0, rs, gs] = (jnp.concatenate(ys, axis=1) + y_off).astype(o_ref.dtype)
            yield


def _ssd_kernel(xf_ref, bf_ref, cf_ref, smf_ref, xb_ref, bb_ref, cb_ref, smb_ref,
                pvf_ref, pvb_ref, eef_ref, eeb_ref, of_ref, ob_ref, state_ref, *, heads):
    @pl.when(pl.program_id(1) == 0)
    def _():
        state_ref[...] = jnp.zeros_like(state_ref)

    _interleave(_ssd_tile(xf_ref, bf_ref, cf_ref, smf_ref, pvf_ref, eef_ref, of_ref, state_ref.at[0],
                          reverse=False, lane_off=0),
                _ssd_tile(xb_ref, bb_ref, cb_ref, smb_ref, pvb_ref, eeb_ref, ob_ref, state_ref.at[1],
                          reverse=True, lane_off=heads))


def _scan_tile(nt, reverse):
    if reverse:
        return lambda s: jnp.where(s == 0, 0, nt - s)
    return lambda s: s


def _ssd(xbc, small, pvecs, ees, width, heads):
    bsz, t, _ = xbc.shape
    nt = t // TILE
    n = SSD_STATE
    gw = width // SSD_GROUPS
    bcb = width // (SSD_GROUPS * n)
    const = lambda b, s: (0, 0)
    in_specs, args = [], []
    for reverse in (False, True):
        tile = _scan_tile(nt, reverse)
        tok = lambda cb, tile=tile: (lambda b, s: (b, tile(s), cb))
        in_specs += [pl.BlockSpec((1, TILE, width), tok(0)),
                     pl.BlockSpec((1, TILE, SSD_GROUPS * n), tok(bcb)),
                     pl.BlockSpec((1, TILE, SSD_GROUPS * n), tok(bcb + 1)),
                     pl.BlockSpec((1, TILE, LANES), tok(0))]
        args += [xbc, xbc, xbc, small]
    in_specs += [pl.BlockSpec(a.shape, const) for a in (*pvecs, *ees)]
    args += [*pvecs, *ees]
    out_specs = [pl.BlockSpec((1, TILE, width), lambda b, s, tile=_scan_tile(nt, rev): (b, tile(s), 0))
                 for rev in (False, True)]
    return pl.pallas_call(
        functools.partial(_ssd_kernel, heads=heads),
        grid=(bsz, nt),
        in_specs=in_specs,
        out_specs=out_specs,
        out_shape=[jax.ShapeDtypeStruct((bsz, t, width), bf16)] * 2,
        scratch_shapes=[pltpu.VMEM((2, SSD_GROUPS, n, gw), f32)],
        compiler_params=_cparams("parallel", "arbitrary"),
        name="ssd",
    )(*args)


def _gla_tile(q_ref, k_ref, v_ref, sm_ref, w2_ref, b2_ref, o_ref, state_ref, *, reverse):
    c = GLA_CHUNK
    dk = state_ref.shape[1]
    dv = state_ref.shape[2]
    kw = GLA_HEADS * dk
    scale = dk ** -0.5

    mask = _chunk_masks(c, reverse)
    tri = mask.astype(bf16)

    gate = jnp.dot(sm_ref[0].astype(bf16), w2_ref[...], preferred_element_type=f32) + b2_ref[...]
    log_a = -_softplus(-gate) * (LOG2E / GLA_GATE_NORM)
    bc = _split_dot(tri, log_a)
    yield
    ends = _chunk_ends(bc, c, reverse)
    b_end = jnp.concatenate([jnp.broadcast_to(e, (c, kw)) for e in ends], axis=0)
    q_dec = q_ref[0] * (jnp.exp2(bc) * scale).astype(bf16)
    k16 = k_ref[0]
    k_inv = k16 * jnp.exp2(-bc).astype(bf16)
    k_end = k16 * jnp.exp2(b_end - bc).astype(bf16)
    dec_cols = [jnp.broadcast_to(jnp.exp2(e), (LANES, kw)).T for e in ends]
    yield

    chunks = range(TILE // c)
    dks = [slice(h * dk, (h + 1) * dk) for h in range(GLA_HEADS)]
    dvs = [slice(h * dv, (h + 1) * dv) for h in range(GLA_HEADS)]
    o_intra = []
    for h in range(GLA_HEADS):
        att = jnp.where(mask, _dot_nt(q_dec[:, dks[h]], k_inv[:, dks[h]]), 0.0).astype(bf16)
        o_intra.append(jnp.dot(att, v_ref[0, :, dvs[h]], preferred_element_type=f32))
        yield
    states = [state_ref[h] for h in range(GLA_HEADS)]
    for ci in (reversed(chunks) if reverse else chunks):
        rs = slice(ci * c, (ci + 1) * c)
        for h in range(GLA_HEADS):
            s = states[h]
            o = o_intra[h][rs] + jnp.dot(q_dec[rs, dks[h]], s.astype(bf16), preferred_element_type=f32)
            o_ref[0, rs, dvs[h]] = o.astype(o_ref.dtype)
            dec = dec_cols[ci][dks[h], :]
            states[h] = (s * jnp.concatenate([dec] * (dv // LANES), axis=1)
                         + _dot_tn(k_end[rs, dks[h]], v_ref[0, rs, dvs[h]]))
            yield
    for h in range(GLA_HEADS):
        state_ref[h] = states[h]


def _gla_kernel(qf_ref, kf_ref, vf_ref, smf_ref, qb_ref, kb_ref, vb_ref, smb_ref,
                w2f_ref, w2b_ref, b2f_ref, b2b_ref, of_ref, ob_ref, state_ref):
    @pl.when(pl.program_id(1) == 0)
    def _():
        state_ref[...] = jnp.zeros_like(state_ref)

    _interleave(_gla_tile(qf_ref, kf_ref, vf_ref, smf_ref, w2f_ref, b2f_ref, of_ref, state_ref.at[0], reverse=False),
                _gla_tile(qb_ref, kb_ref, vb_ref, smb_ref, w2b_ref, b2b_ref, ob_ref, state_ref.at[1], reverse=True))


def _gla(qk, v, small, w2ps, b2s):
    bsz, t, kw2 = qk.shape
    kw = kw2 // 2
    vw = v.shape[-1]
    nt = t // TILE
    const = lambda b, s: (0, 0)
    in_specs, args = [], []
    for reverse in (False, True):
        tile = _scan_tile(nt, reverse)
        tok = lambda cb, tile=tile: (lambda b, s: (b, tile(s), cb))
        in_specs += [pl.BlockSpec((1, TILE, kw), tok(0)),
                     pl.BlockSpec((1, TILE, kw), tok(1)),
                     pl.BlockSpec((1, TILE, vw), tok(0)),
                     pl.BlockSpec((1, TILE, LANES), tok(0))]
        args += [qk, qk, v, small]
    in_specs += [pl.BlockSpec(a.shape, const) for a in (*w2ps, *b2s)]
    args += [*w2ps, *b2s]
    out_specs = [pl.BlockSpec((1, TILE, vw), lambda b, s, tile=_scan_tile(nt, rev): (b, tile(s), 0))
                 for rev in (False, True)]
    return pl.pallas_call(
        _gla_kernel,
        grid=(bsz, nt),
        in_specs=in_specs,
        out_specs=out_specs,
        out_shape=[jax.ShapeDtypeStruct((bsz, t, vw), bf16)] * 2,
        scratch_shapes=[pltpu.VMEM((2, GLA_HEADS, kw // GLA_HEADS, vw // GLA_HEADS), f32)],
        compiler_params=_cparams("parallel", "arbitrary"),
        name="gla",
    )(*args)


def _outmlp_kernel(*refs, n_stream, ff_chunk, final):
    (yf_ref, yb_ref, xs_ref, z_ref, sg_ref, dexp_ref, of_ref, ob_ref, r_ref, gg_ref,
     mod_ref, wo_ref, g2_ref, w1_ref, w2_ref) = refs[n_stream:n_stream + 15]
    o_ref = refs[-1]
    m = mod_ref[0]

    sw = yf_ref.shape[-1]
    gw = sw // SSD_GROUPS
    y = (yf_ref[0] + yb_ref[0]).astype(f32) + dexp_ref[...] * xs_ref[0].astype(f32)
    y = y * _silu(z_ref[0]).astype(f32)
    mix = None
    for g in range(SSD_GROUPS):
        yg = y[:, g * gw:(g + 1) * gw]
        ms = jnp.mean(yg * yg, axis=-1, keepdims=True)
        part = (yg * lax.rsqrt(ms + EPS) * sg_ref[:, g * gw:(g + 1) * gw]).astype(bf16)
        term = jnp.dot(part, wo_ref[g * gw:(g + 1) * gw, :], preferred_element_type=f32)
        mix = term if mix is None else mix + term
    dv = gg_ref.shape[-1]
    o = (of_ref[0] + ob_ref[0]).astype(f32)
    gate = _silu(r_ref[0]).astype(f32)
    for hh in range(GLA_HEADS):
        oh = o[:, hh * dv:(hh + 1) * dv]
        ms = jnp.mean(oh * oh, axis=-1, keepdims=True)
        part = (oh * lax.rsqrt(ms + EPS) * gg_ref[...] * gate[:, hh * dv:(hh + 1) * dv]).astype(bf16)
        mix = mix + jnp.dot(part, wo_ref[sw + hh * dv:sw + (hh + 1) * dv, :], preferred_element_type=f32)
    h1 = _read_stream(refs[:n_stream]) + m[2:3] * mix
    ms = jnp.mean(h1 * h1, axis=-1, keepdims=True)
    u2 = (h1 * lax.rsqrt(ms + EPS) * (g2_ref[...] * (1.0 + m[4:5])) + m[3:4]).astype(bf16)
    acc = jnp.zeros_like(h1)
    for c0 in range(0, w1_ref.shape[-1], ff_chunk):
        hid = jnp.maximum(jnp.dot(u2, w1_ref[:, c0:c0 + ff_chunk], preferred_element_type=f32), 0.0)
        acc = acc + jnp.dot((hid * hid).astype(bf16), w2_ref[c0:c0 + ff_chunk, :], preferred_element_type=f32)
    h2 = h1 + m[5:6] * acc
    if final:
        fg_ref = refs[n_stream + 15]
        ms2 = jnp.mean(h2 * h2, axis=-1, keepdims=True)
        h2 = h2 * lax.rsqrt(ms2 + EPS) * fg_ref[...]
    o_ref[0] = h2


def _outmlp(h, ssd_args, gla_args, mod, wo, g2, w1, w2, ctx_row, final_gain=None):
    yf, yb, xbc, z, sg, dexp = ssd_args
    of, ob, r, gg = gla_args
    bsz, t, sw = yf.shape
    vw = of.shape[-1]
    d = wo.shape[-1]
    nt = t // TILE
    final = final_gain is not None
    skip = 1 if final else 0
    assert not (final and isinstance(h, tuple))
    tok = lambda b, j: (b, j + skip, 0)
    const = lambda b, j: (0, 0)
    resident = lambda a: pl.BlockSpec(a.shape, const, pipeline_mode=pl.Buffered(1))
    if isinstance(h, tuple):
        specs, args = _stream_specs(h, d)
    else:
        specs, args = [pl.BlockSpec((1, TILE, d), tok)], [h]
    n_stream = len(args)
    in_specs = specs + [pl.BlockSpec((1, TILE, sw), tok)] * 4 + [pl.BlockSpec((1, sw), const)] * 2
    in_specs += [pl.BlockSpec((1, TILE, vw), tok)] * 3 + [pl.BlockSpec(gg.shape, const)]
    in_specs += [pl.BlockSpec((1, 6, d), lambda b, j: (jnp.where(j + skip == 0, ctx_row, b), 0, 0)),
                 resident(wo), pl.BlockSpec((1, d), const), resident(w1), resident(w2)]
    args = args + [yf, yb, xbc, z, sg, dexp, of, ob, r, gg, mod, wo, g2, w1, w2]
    if final:
        in_specs.append(pl.BlockSpec((1, d), const))
        args.append(final_gain)
    return pl.pallas_call(
        functools.partial(_outmlp_kernel, n_stream=n_stream, ff_chunk=1024, final=final),
        grid=(bsz, nt - skip),
        in_specs=in_specs,
        out_specs=pl.BlockSpec((1, TILE, d), lambda b, j: (b, j, 0)),
        out_shape=jax.ShapeDtypeStruct((bsz, t - skip * TILE, d), f32),
        compiler_params=_cparams("parallel", "arbitrary"),
        name="outmlp_final" if final else "outmlp",
    )(*args)


def kernel(x, c, ctx, c_ctx, w_ada, b_ada, norm1_g, w_in, conv_w, conv_b, dt_bias, a_log, d_skip,
           ssd_norm_g, gla_w2, gla_b2, gla_norm_g, w_out, norm2_g, w_ff1, w_ff2, final_norm_g):
    bsz, n_lat, d = x.shape
    ctx_len = ctx.shape[1]
    t = ctx_len + n_lat
    depth = w_in.shape[0]
    ssd_w = ssd_norm_g.shape[-1]
    ssd_heads = dt_bias.shape[-1]
    gla_kw = gla_w2.shape[-1]
    gla_dv = gla_norm_g.shape[-1]
    gla_vw = GLA_HEADS * gla_dv
    bc_w = SSD_GROUPS * SSD_STATE
    rank = GLA_GATE_RANK
    assert ctx_len == TILE and n_lat % TILE == 0 and TILE % GRID_W == 0 and n_lat > TILE
    assert ssd_w == ssd_heads * SSD_HEAD_DIM and 2 * ssd_heads + 2 * rank <= LANES
    assert w_in.shape[-1] == 2 * ssd_w + 2 * bc_w + 2 * ssd_heads + 2 * gla_kw + 2 * gla_vw + 2 * rank

    n_rows = -(-(bsz + 1) // SUBLANES) * SUBLANES
    cc = jnp.zeros((n_rows, d), f32).at[:bsz].set(c).at[bsz].set(c_ctx)
    mods = _ada(cc, w_ada, b_ada).reshape(depth, n_rows, 6, d)

    o_dt = 2 * ssd_w + 2 * bc_w
    o_q = o_dt + 2 * ssd_heads
    o_gate = o_q + 2 * gla_kw + 2 * gla_vw
    widths = (ssd_w, ssd_w + 2 * bc_w, 2 * gla_kw, gla_vw, gla_vw, LANES)
    w_in_t = jnp.swapaxes(w_in, 1, 2)
    segments = ((0, o_dt), (o_q, o_gate), (o_dt, o_q), (o_gate, o_gate + 2 * rank))

    def expander(lane_off):
        lanes = jnp.arange(LANES)[:, None]
        heads = jnp.arange(ssd_w)[None, :] // SSD_HEAD_DIM
        e = (lanes == heads + lane_off).astype(bf16)
        return jnp.concatenate([e, e], axis=0)

    def lane_row(vals, off):
        return jnp.zeros((LANES,), f32).at[off:off + vals.shape[0]].set(vals)

    h = (ctx, x)
    for l in range(depth):
        z, xbc, qk, v, r, small = _inproj(h, mods[l], norm1_g[l][None], w_in_t, l, segments,
                                          conv_w[l].reshape(CONV_K * CONV_K, -1), conv_b[l][None],
                                          widths, bsz, bsz, t, GRID_W)

        pvecs, ees = [], []
        for dirn in range(2):
            off = dirn * ssd_heads
            pvec = jnp.zeros((SUBLANES, LANES), f32)
            pvec = pvec.at[0].set(lane_row(dt_bias[l, dirn], off)).at[1].set(lane_row(a_log[l, dirn], off))
            pvecs.append(pvec.at[2].set(lane_row(jnp.ones((ssd_heads,), f32), off)))
            ees.append(expander(off))
        y_f, y_b = _ssd(xbc, small, pvecs, ees, ssd_w, ssd_heads)

        w2ps = []
        for dirn in range(2):
            off = 2 * ssd_heads + dirn * rank
            w2ps.append(jnp.zeros((LANES, gla_kw), f32).at[off:off + rank].set(gla_w2[l, dirn]).astype(bf16))
        o_f, o_b = _gla(qk, v, small, w2ps, [gla_b2[l, 0][None], gla_b2[l, 1][None]])

        last = l == depth - 1
        dexp = jnp.repeat(d_skip[l, 0] + d_skip[l, 1], SSD_HEAD_DIM)[None]
        h = _outmlp(h, (y_f, y_b, xbc, z, ssd_norm_g[l][None], dexp), (o_f, o_b, r, gla_norm_g[l][None]),
                    mods[l], w_out[l].astype(bf16), norm2_g[l][None],
                    w_ff1[l].astype(bf16), w_ff2[l].astype(bf16), bsz,
                    final_norm_g[None] if last else None)
    return h
```

```python
import functools

import jax
import jax.numpy as jnp
from jax import lax
from jax.experimental import pallas as pl
from jax.experimental.pallas import tpu as pltpu

f32 = jnp.float32
bf16 = jnp.bfloat16

GRID_W = 64
SSD_HEAD_DIM = 64
SSD_GROUPS = 2
SSD_STATE = 128
SSD_CHUNK = 128
CONV_K = 3
GLA_HEADS = 4
GLA_GATE_RANK = 16
GLA_GATE_NORM = 16.0
GLA_CHUNK = 64
EPS = 1e-6

TILE = 256
LANES = 128
SUBLANES = 8
VMEM_LIMIT = 56 * 1024 * 1024
LOG2E = 1.4426950408889634


def _cparams(*sem):
    return pltpu.CompilerParams(dimension_semantics=sem, vmem_limit_bytes=VMEM_LIMIT)


def _softplus(x):
    return jnp.maximum(x, 0.0) + jnp.log(1.0 + jnp.exp2(-LOG2E * jnp.abs(x)))


def _silu(x):
    if x.dtype == bf16:
        return x / (1.0 + jnp.exp(-x))
    return x / (1.0 + jnp.exp2(-LOG2E * x))


def _split_dot(lhs_bf16, x):
    hi = x.astype(bf16)
    lo = (x - hi.astype(f32)).astype(bf16)
    return jnp.dot(jnp.concatenate([lhs_bf16, lhs_bf16], axis=1), jnp.concatenate([hi, lo], axis=0),
                   preferred_element_type=f32)


def _dot_nt(a, b):
    return lax.dot_general(a, b, (((1,), (1,)), ((), ())), preferred_element_type=f32)


def _dot_tn(a, b):
    return lax.dot_general(a, b, (((0,), (0,)), ((), ())), preferred_element_type=f32)


def _chunk_masks(chunk, reverse):
    ii = lax.broadcasted_iota(jnp.int32, (TILE, TILE), 0)
    jj = lax.broadcasted_iota(jnp.int32, (TILE, TILE), 1)
    same = (ii // chunk) == (jj // chunk)
    return same & ((jj >= ii) if reverse else (jj <= ii))


def _chunk_ends(cum, chunk, reverse):
    n = TILE // chunk
    return [cum[ci * chunk:ci * chunk + 1] if reverse else cum[(ci + 1) * chunk - 1:(ci + 1) * chunk]
            for ci in range(n)]


def _interleave(*streams):
    live = list(streams)
    while live:
        for s in list(live):
            if next(s, StopIteration) is StopIteration:
                live.remove(s)


def _ada_kernel(cc_ref, w_ref, b_ref, o_ref):
    s = _silu(cc_ref[...]).astype(bf16)
    o_ref[0] = jnp.dot(s, w_ref[0].astype(bf16), preferred_element_type=f32) + b_ref[0]


def _ada(cc, w_ada, b_ada):
    depth, d, n = w_ada.shape
    tn = n // 4
    rows = cc.shape[0]
    return pl.pallas_call(
        _ada_kernel,
        grid=(depth, n // tn),
        in_specs=[pl.BlockSpec((rows, d), lambda l, j: (0, 0)),
                  pl.BlockSpec((1, d, tn), lambda l, j: (l, 0, j)),
                  pl.BlockSpec((1, 1, tn), lambda l, j: (l, 0, j))],
        out_specs=pl.BlockSpec((1, rows, tn), lambda l, j: (l, 0, j)),
        out_shape=jax.ShapeDtypeStruct((depth, rows, n), f32),
        compiler_params=_cparams("arbitrary", "arbitrary"),
        name="ada",
    )(cc, w_ada, b_ada.reshape(depth, 1, n))


def _stream_specs(h, d):
    if isinstance(h, tuple):
        ctx, x = h
        return ([pl.BlockSpec((1, TILE, d), lambda b, j: (b, 0, 0)),
                 pl.BlockSpec((1, TILE, d), lambda b, j: (b, jnp.maximum(j - 1, 0), 0))], [ctx, x])
    return [pl.BlockSpec((1, TILE, d), lambda b, j: (b, j, 0))], [h]


def _read_stream(refs):
    if len(refs) == 2:
        return jnp.where(pl.program_id(1) == 0, refs[0][0], refs[1][0])
    return refs[0][0]


def _inproj_kernel(*refs, n_stream, cols, nt, segments):
    up_ref, dn_ref, mod_ref, g_ref, win_ref, cw_ref, cb_ref = refs[n_stream:n_stream + 7]
    z_ref, xbc_ref, qk_ref, v_ref, r_ref, sm_ref, w_ref = refs[n_stream + 7:]
    j = pl.program_id(1)

    @pl.when((pl.program_id(0) == 0) & (j == 0))
    def _():
        dst = 0
        for lo, hi in segments:
            w_ref[dst:dst + hi - lo, :] = win_ref[0, lo:hi, :].astype(bf16)
            dst += hi - lo
        w_ref[dst:, :] = jnp.zeros((w_ref.shape[0] - dst, w_ref.shape[1]), bf16)
    m = mod_ref[0]
    gain = g_ref[...] * (1.0 + m[1:2])

    def normed(x):
        ms = jnp.mean(x * x, axis=-1, keepdims=True)
        return (x * lax.rsqrt(ms + EPS) * gain + m[0:1]).astype(bf16)

    u = normed(_read_stream(refs[:n_stream]))
    u_ext = jnp.concatenate([normed(up_ref[0]), u, normed(dn_ref[0])], axis=0)

    zw = z_ref.shape[-1]
    cw = xbc_ref.shape[-1]
    xe = _dot_nt(u_ext, w_ref[zw:zw + cw, :])

    is_ctx = j == 0
    lat = jnp.where(is_ctx, 0.0, 1.0)
    top = jnp.where(j <= 1, 0.0, 1.0)
    bot = jnp.where(is_ctx | (j == nt - 1), 0.0, 1.0)
    w = cw_ref[...]
    rows_per_tile = TILE // cols
    xe = xe.astype(bf16)
    parts = []
    for kj in range(CONV_K):
        blocks = []
        for lr in range(rows_per_tile):
            f_up = top if lr == 0 else lat
            f_dn = bot if lr == rows_per_tile - 1 else lat
            blocks.append(xe[lr * cols:(lr + 1) * cols] * (w[kj:kj + 1] * f_up).astype(bf16)
                          + xe[(lr + 1) * cols:(lr + 2) * cols] * w[CONV_K + kj:CONV_K + kj + 1].astype(bf16)
                          + xe[(lr + 2) * cols:(lr + 3) * cols] * (w[2 * CONV_K + kj:2 * CONV_K + kj + 1] * f_dn).astype(bf16))
        parts.append(jnp.concatenate(blocks, axis=0).astype(f32))

    sub = lax.broadcasted_iota(jnp.int32, (SUBLANES, cw), 0)
    rl = pltpu.roll(parts[0], 1, 0)
    rr = pltpu.roll(parts[2], TILE - 1, 0)
    out_blocks = []
    for t0 in range(0, TILE, SUBLANES):
        lb = rl[t0:t0 + SUBLANES]
        rb = rr[t0:t0 + SUBLANES]
        if t0 % cols == 0:
            keep = (sub >= 1) if t0 == 0 else ((sub >= 1) | is_ctx)
            lb = jnp.where(keep, lb, 0.0)
        if (t0 + SUBLANES) % cols == 0:
            keep = (sub <= SUBLANES - 2) if t0 + SUBLANES == TILE else ((sub <= SUBLANES - 2) | is_ctx)
            rb = jnp.where(keep, rb, 0.0)
        out_blocks.append(lb + rb)
    conv = jnp.concatenate(out_blocks, axis=0) + parts[1] + cb_ref[...]
    xbc_ref[0] = _silu(conv).astype(xbc_ref.dtype)

    col = 0
    for ref in (z_ref, xbc_ref, qk_ref, v_ref, r_ref, sm_ref):
        width = ref.shape[-1]
        if ref is not xbc_ref:
            ref[0] = _dot_nt(u, w_ref[col:col + width, :]).astype(ref.dtype)
        col += width


def _inproj(h, mod, g, w_in, layer, segments, cw9, cbias, widths, ctx_row, bsz, t, cols):
    d = w_in.shape[2]
    nt = t // TILE
    rpt = TILE // cols
    tok = lambda b, j: (b, j, 0)
    specs, args = _stream_specs(h, d)
    if isinstance(h, tuple):
        lat, first, last = h[1], 0, (t - TILE) // cols - 1
    else:
        lat, first, last = h, rpt, t // cols - 1
    up = lambda b, j: (b, jnp.clip(first + (j - 1) * rpt - 1, first, last), 0)
    dn = lambda b, j: (b, jnp.clip(first + j * rpt, first, last), 0)
    const = lambda b, j: (0, 0)
    out_shape = [jax.ShapeDtypeStruct((bsz, t, wd), bf16) for wd in widths[:-1]]
    out_shape.append(jax.ShapeDtypeStruct((bsz, t, widths[-1]), f32))
    return pl.pallas_call(
        functools.partial(_inproj_kernel, n_stream=len(args), cols=cols, nt=nt, segments=segments),
        grid=(bsz, nt),
        in_specs=specs + [pl.BlockSpec((1, cols, d), up), pl.BlockSpec((1, cols, d), dn),
                          pl.BlockSpec((1, 6, d), lambda b, j: (jnp.where(j == 0, ctx_row, b), 0, 0)),
                          pl.BlockSpec((1, d), const),
                          pl.BlockSpec((1,) + w_in.shape[1:], lambda b, j: (layer, 0, 0),
                                       pipeline_mode=pl.Buffered(1)),
                          pl.BlockSpec(cw9.shape, const),
                          pl.BlockSpec(cbias.shape, const)],
        out_specs=[pl.BlockSpec((1, TILE, wd), tok) for wd in widths],
        out_shape=out_shape,
        scratch_shapes=[pltpu.VMEM((sum(widths), d), bf16)],
        compiler_params=_cparams("arbitrary", "arbitrary"),
        name="inproj",
    )(*args, lat, lat, mod, g, w_in, cw9, cbias)


def _ssd_tile(x_ref, b_ref, c_ref, sm_ref, pv_ref, ee_ref, o_ref, state_ref, *, reverse, lane_off):
    q = SSD_CHUNK
    n = SSD_STATE
    gw = state_ref.shape[-1]
    pairs_per_group = gw // LANES
    hd = SSD_HEAD_DIM

    tri = _chunk_masks(q, reverse).astype(bf16)
    ci_ = lax.broadcasted_iota(jnp.int32, (q, q), 0)
    cj_ = lax.broadcasted_iota(jnp.int32, (q, q), 1)
    mask = (cj_ >= ci_) if reverse else (cj_ <= ci_)
    lo_half = lax.broadcasted_iota(jnp.int32, (q, LANES), 1) < hd
    bias = pv_ref[0:1, :]
    a_coef = -jnp.exp(pv_ref[1:2, :]) * pv_ref[2:3, :] * LOG2E
    ee = ee_ref[...]

    def expand(v, cols):
        hi = v.astype(bf16)
        lo = (v - hi.astype(f32)).astype(bf16)
        return jnp.dot(jnp.concatenate([hi, lo], axis=1), ee[:, cols], preferred_element_type=f32)

    dt = _softplus(sm_ref[0] + bias)
    ac = _split_dot(tri, dt * a_coef)
    ends = _chunk_ends(ac, q, reverse)
    ac_end = jnp.concatenate([jnp.broadcast_to(e, (q, LANES)) for e in ends], axis=0)
    wgt = dt * jnp.exp2(ac_end - ac)
    dcy = jnp.exp2(ac)
    src_t = (ac - jnp.log2(dt)).T
    yield

    chunks = range(TILE // q)
    for ci in (reversed(chunks) if reverse else chunks):
        rs = slice(ci * q, (ci + 1) * q)
        end_row = ci * q if reverse else (ci + 1) * q - 1
        xs = x_ref[0, rs, :]
        for g in range(SSD_GROUPS):
            gs = slice(g * gw, (g + 1) * gw)
            xw_g = xs[:, gs] * expand(wgt[rs], gs).astype(bf16)
            eexp_g = expand(dcy[rs], gs)
            bg = b_ref[0, rs, g * n:(g + 1) * n]
            cg = c_ref[0, rs, g * n:(g + 1) * n]
            cbm16 = _dot_nt(cg, bg).astype(bf16)
            s_g = state_ref[g]
            y_off = jnp.dot(cg, s_g.astype(bf16), preferred_element_type=f32) * eexp_g
            ys = []
            for pp in range(pairs_per_group):
                p = g * pairs_per_group + pp
                pieces = []
                for h in (2 * p, 2 * p + 1):
                    li = lane_off + h
                    colb = jnp.broadcast_to(ac[rs, li:li + 1], (q, q))
                    dec = jnp.exp2(colb - src_t[li:li + 1, rs]).astype(bf16)
                    pieces.append(cbm16 * jnp.where(mask, dec, jnp.zeros_like(dec)))
                lhs = jnp.concatenate(pieces, axis=1)
                xp = xs[:, p * LANES:(p + 1) * LANES]
                zero = jnp.zeros_like(xp)
                rhs = jnp.concatenate([jnp.where(lo_half, xp, zero), jnp.where(lo_half, zero, xp)], axis=0)
                ys.append(jnp.dot(lhs, rhs, preferred_element_type=f32))
                yield
            state_ref[g] = s_g * eexp_g[end_row - ci * q:end_row - ci * q + 1] + _dot_tn(bg, xw_g)
            o_ref[0, rs, gs] = (jnp.concatenate(ys, axis=1) + y_off).astype(o_ref.dtype)
            yield


def _ssd_kernel(xf_ref, bf_ref, cf_ref, smf_ref, xb_ref, bb_ref, cb_ref, smb_ref,
                pvf_ref, pvb_ref, eef_ref, eeb_ref, of_ref, ob_ref, state_ref, *, heads):
    @pl.when(pl.program_id(1) == 0)
    def _():
        state_ref[...] = jnp.zeros_like(state_ref)

    _interleave(_ssd_tile(xf_ref, bf_ref, cf_ref, smf_ref, pvf_ref, eef_ref, of_ref, state_ref.at[0],
                          reverse=False, lane_off=0),
                _ssd_tile(xb_ref, bb_ref, cb_ref, smb_ref, pvb_ref, eeb_ref, ob_ref, state_ref.at[1],
                          reverse=True, lane_off=heads))


def _scan_tile(nt, reverse):
    if reverse:
        return lambda s: jnp.where(s == 0, 0, nt - s)
    return lambda s: s


def _ssd(xbc, small, pvecs, ees, width, heads):
    bsz, t, _ = xbc.shape
    nt = t // TILE
    n = SSD_STATE
    gw = width // SSD_GROUPS
    bcb = width // (SSD_GROUPS * n)
    const = lambda b, s: (0, 0)
    in_specs, args = [], []
    for reverse in (False, True):
        tile = _scan_tile(nt, reverse)
        tok = lambda cb, tile=tile: (lambda b, s: (b, tile(s), cb))
        in_specs += [pl.BlockSpec((1, TILE, width), tok(0)),
                     pl.BlockSpec((1, TILE, SSD_GROUPS * n), tok(bcb)),
                     pl.BlockSpec((1, TILE, SSD_GROUPS * n), tok(bcb + 1)),
                     pl.BlockSpec((1, TILE, LANES), tok(0))]
        args += [xbc, xbc, xbc, small]
    in_specs += [pl.BlockSpec(a.shape, const) for a in (*pvecs, *ees)]
    args += [*pvecs, *ees]
    out_specs = [pl.BlockSpec((1, TILE, width), lambda b, s, tile=_scan_tile(nt, rev): (b, tile(s), 0))
                 for rev in (False, True)]
    return pl.pallas_call(
        functools.partial(_ssd_kernel, heads=heads),
        grid=(bsz, nt),
        in_specs=in_specs,
        out_specs=out_specs,
        out_shape=[jax.ShapeDtypeStruct((bsz, t, width), bf16)] * 2,
        scratch_shapes=[pltpu.VMEM((2, SSD_GROUPS, n, gw), f32)],
        compiler_params=_cparams("parallel", "arbitrary"),
        name="ssd",
    )(*args)


def _gla_tile(q_ref, k_ref, v_ref, sm_ref, w2_ref, b2_ref, o_ref, state_ref, *, reverse):
    c = GLA_CHUNK
    dk = state_ref.shape[1]
    dv = state_ref.shape[2]
    kw = GLA_HEADS * dk
    scale = dk ** -0.5

    mask = _chunk_masks(c, reverse)
    tri = mask.astype(bf16)

    gate = jnp.dot(sm_ref[0].astype(bf16), w2_ref[...], preferred_element_type=f32) + b2_ref[...]
    log_a = -_softplus(-gate) * (LOG2E / GLA_GATE_NORM)
    bc = _split_dot(tri, log_a)
    yield
    ends = _chunk_ends(bc, c, reverse)
    b_end = jnp.concatenate([jnp.broadcast_to(e, (c, kw)) for e in ends], axis=0)
    q_dec = q_ref[0] * (jnp.exp2(bc) * scale).astype(bf16)
    k16 = k_ref[0]
    k_inv = k16 * jnp.exp2(-bc).astype(bf16)
    k_end = k16 * jnp.exp2(b_end - bc).astype(bf16)
    dec_cols = [jnp.broadcast_to(jnp.exp2(e), (LANES, kw)).T for e in ends]
    yield

    chunks = range(TILE // c)
    dks = [slice(h * dk, (h + 1) * dk) for h in range(GLA_HEADS)]
    dvs = [slice(h * dv, (h + 1) * dv) for h in range(GLA_HEADS)]
    o_intra = []
    for h in range(GLA_HEADS):
        att = jnp.where(mask, _dot_nt(q_dec[:, dks[h]], k_inv[:, dks[h]]), 0.0).astype(bf16)
        o_intra.append(jnp.dot(att, v_ref[0, :, dvs[h]], preferred_element_type=f32))
        yield
    states = [state_ref[h] for h in range(GLA_HEADS)]
    for ci in (reversed(chunks) if reverse else chunks):
        rs = slice(ci * c, (ci + 1) * c)
        for h in range(GLA_HEADS):
            s = states[h]
            o = o_intra[h][rs] + jnp.dot(q_dec[rs, dks[h]], s.astype(bf16), preferred_element_type=f32)
            o_ref[0, rs, dvs[h]] = o.astype(o_ref.dtype)
            dec = dec_cols[ci][dks[h], :]
            states[h] = (s * jnp.concatenate([dec] * (dv // LANES), axis=1)
                         + _dot_tn(k_end[rs, dks[h]], v_ref[0, rs, dvs[h]]))
            yield
    for h in range(GLA_HEADS):
        state_ref[h] = states[h]


def _gla_kernel(qf_ref, kf_ref, vf_ref, smf_ref, qb_ref, kb_ref, vb_ref, smb_ref,
                w2f_ref, w2b_ref, b2f_ref, b2b_ref, of_ref, ob_ref, state_ref):
    @pl.when(pl.program_id(1) == 0)
    def _():
        state_ref[...] = jnp.zeros_like(state_ref)

    _interleave(_gla_tile(qf_ref, kf_ref, vf_ref, smf_ref, w2f_ref, b2f_ref, of_ref, state_ref.at[0], reverse=False),
                _gla_tile(qb_ref, kb_ref, vb_ref, smb_ref, w2b_ref, b2b_ref, ob_ref, state_ref.at[1], reverse=True))


def _gla(qk, v, small, w2ps, b2s):
    bsz, t, kw2 = qk.shape
    kw = kw2 // 2
    vw = v.shape[-1]
    nt = t // TILE
    const = lambda b, s: (0, 0)
    in_specs, args = [], []
    for reverse in (False, True):
        tile = _scan_tile(nt, reverse)
        tok = lambda cb, tile=tile: (lambda b, s: (b, tile(s), cb))
        in_specs += [pl.BlockSpec((1, TILE, kw), tok(0)),
                     pl.BlockSpec((1, TILE, kw), tok(1)),
                     pl.BlockSpec((1, TILE, vw), tok(0)),
                     pl.BlockSpec((1, TILE, LANES), tok(0))]
        args += [qk, qk, v, small]
    in_specs += [pl.BlockSpec(a.shape, const) for a in (*w2ps, *b2s)]
    args += [*w2ps, *b2s]
    out_specs = [pl.BlockSpec((1, TILE, vw), lambda b, s, tile=_scan_tile(nt, rev): (b, tile(s), 0))
                 for rev in (False, True)]
    return pl.pallas_call(
        _gla_kernel,
        grid=(bsz, nt),
        in_specs=in_specs,
        out_specs=out_specs,
        out_shape=[jax.ShapeDtypeStruct((bsz, t, vw), bf16)] * 2,
        scratch_shapes=[pltpu.VMEM((2, GLA_HEADS, kw // GLA_HEADS, vw // GLA_HEADS), f32)],
        compiler_params=_cparams("parallel", "arbitrary"),
        name="gla",
    )(*args)


def _outmlp_kernel(*refs, n_stream, ff_chunk, final):
    (yf_ref, yb_ref, xs_ref, z_ref, sg_ref, dexp_ref, of_ref, ob_ref, r_ref, gg_ref,
     mod_ref, wo_ref, g2_ref, w1_ref, w2_ref) = refs[n_stream:n_stream + 15]
    o_ref = refs[-1]
    m = mod_ref[0]

    sw = yf_ref.shape[-1]
    gw = sw // SSD_GROUPS
    y = (yf_ref[0] + yb_ref[0]).astype(f32) + dexp_ref[...] * xs_ref[0].astype(f32)
    y = y * _silu(z_ref[0]).astype(f32)
    mix = None
    for g in range(SSD_GROUPS):
        yg = y[:, g * gw:(g + 1) * gw]
        ms = jnp.mean(yg * yg, axis=-1, keepdims=True)
        part = (yg * lax.rsqrt(ms + EPS) * sg_ref[:, g * gw:(g + 1) * gw]).astype(bf16)
        term = jnp.dot(part, wo_ref[g * gw:(g + 1) * gw, :], preferred_element_type=f32)
        mix = term if mix is None else mix + term
    dv = gg_ref.shape[-1]
    o = (of_ref[0] + ob_ref[0]).astype(f32)
    gate = _silu(r_ref[0]).astype(f32)
    for hh in range(GLA_HEADS):
        oh = o[:, hh * dv:(hh + 1) * dv]
        ms = jnp.mean(oh * oh, axis=-1, keepdims=True)
        part = (oh * lax.rsqrt(ms + EPS) * gg_ref[...] * gate[:, hh * dv:(hh + 1) * dv]).astype(bf16)
        mix = mix + jnp.dot(part, wo_ref[sw + hh * dv:sw + (hh + 1) * dv, :], preferred_element_type=f32)
    h1 = _read_stream(refs[:n_stream]) + m[2:3] * mix
    ms = jnp.mean(h1 * h1, axis=-1, keepdims=True)
    u2 = (h1 * lax.rsqrt(ms + EPS) * (g2_ref[...] * (1.0 + m[4:5])) + m[3:4]).astype(bf16)
    acc = jnp.zeros_like(h1)
    for c0 in range(0, w1_ref.shape[-1], ff_chunk):
        hid = jnp.maximum(jnp.dot(u2, w1_ref[:, c0:c0 + ff_chunk], preferred_element_type=f32), 0.0)
        acc = acc + jnp.dot((hid * hid).astype(bf16), w2_ref[c0:c0 + ff_chunk, :], preferred_element_type=f32)
    h2 = h1 + m[5:6] * acc
    if final:
        fg_ref = refs[n_stream + 15]
        ms2 = jnp.mean(h2 * h2, axis=-1, keepdims=True)
        h2 = h2 * lax.rsqrt(ms2 + EPS) * fg_ref[...]
    o_ref[0] = h2


def _outmlp(h, ssd_args, gla_args, mod, wo, g2, w1, w2, ctx_row, final_gain=None):
    yf, yb, xbc, z, sg, dexp = ssd_args
    of, ob, r, gg = gla_args
    bsz, t, sw = yf.shape
    vw = of.shape[-1]
    d = wo.shape[-1]
    nt = t // TILE
    final = final_gain is not None
    skip = 1 if final else 0
    assert not (final and isinstance(h, tuple))
    tok = lambda b, j: (b, j + skip, 0)
    const = lambda b, j: (0, 0)
    resident = lambda a: pl.BlockSpec(a.shape, const, pipeline_mode=pl.Buffered(1))
    if isinstance(h, tuple):
        specs, args = _stream_specs(h, d)
    else:
        specs, args = [pl.BlockSpec((1, TILE, d), tok)], [h]
    n_stream = len(args)
    in_specs = specs + [pl.BlockSpec((1, TILE, sw), tok)] * 4 + [pl.BlockSpec((1, sw), const)] * 2
    in_specs += [pl.BlockSpec((1, TILE, vw), tok)] * 3 + [pl.BlockSpec(gg.shape, const)]
    in_specs += [pl.BlockSpec((1, 6, d), lambda b, j: (jnp.where(j + skip == 0, ctx_row, b), 0, 0)),
                 resident(wo), pl.BlockSpec((1, d), const), resident(w1), resident(w2)]
    args = args + [yf, yb, xbc, z, sg, dexp, of, ob, r, gg, mod, wo, g2, w1, w2]
    if final:
        in_specs.append(pl.BlockSpec((1, d), const))
        args.append(final_gain)
    return pl.pallas_call(
        functools.partial(_outmlp_kernel, n_stream=n_stream, ff_chunk=1024, final=final),
        grid=(bsz, nt - skip),
        in_specs=in_specs,
        out_specs=pl.BlockSpec((1, TILE, d), lambda b, j: (b, j, 0)),
        out_shape=jax.ShapeDtypeStruct((bsz, t - skip * TILE, d), f32),
        compiler_params=_cparams("parallel", "arbitrary"),
        name="outmlp_final" if final else "outmlp",
    )(*args)


def kernel(x, c, ctx, c_ctx, w_ada, b_ada, norm1_g, w_in, conv_w, conv_b, dt_bias, a_log, d_skip,
           ssd_norm_g, gla_w2, gla_b2, gla_norm_g, w_out, norm2_g, w_ff1, w_ff2, final_norm_g):
    bsz, n_lat, d = x.shape
    ctx_len = ctx.shape[1]
    t = ctx_len + n_lat
    depth = w_in.shape[0]
    ssd_w = ssd_norm_g.shape[-1]
    ssd_heads = dt_bias.shape[-1]
    gla_kw = gla_w2.shape[-1]
    gla_dv = gla_norm_g.shape[-1]
    gla_vw = GLA_HEADS * gla_dv
    bc_w = SSD_GROUPS * SSD_STATE
    rank = GLA_GATE_RANK
    assert ctx_len == TILE and n_lat % TILE == 0 and TILE % GRID_W == 0 and n_lat > TILE
    assert ssd_w == ssd_heads * SSD_HEAD_DIM and 2 * ssd_heads + 2 * rank <= LANES
    assert w_in.shape[-1] == 2 * ssd_w + 2 * bc_w + 2 * ssd_heads + 2 * gla_kw + 2 * gla_vw + 2 * rank

    n_rows = -(-(bsz + 1) // SUBLANES) * SUBLANES
    cc = jnp.zeros((n_rows, d), f32).at[:bsz].set(c).at[bsz].set(c_ctx)
    mods = _ada(cc, w_ada, b_ada).reshape(depth, n_rows, 6, d)

    o_dt = 2 * ssd_w + 2 * bc_w
    o_q = o_dt + 2 * ssd_heads
    o_gate = o_q + 2 * gla_kw + 2 * gla_vw
    widths = (ssd_w, ssd_w + 2 * bc_w, 2 * gla_kw, gla_vw, gla_vw, LANES)
    w_in_t = jnp.swapaxes(w_in, 1, 2)
    segments = ((0, o_dt), (o_q, o_gate), (o_dt, o_q), (o_gate, o_gate + 2 * rank))

    def expander(lane_off):
        lanes = jnp.arange(LANES)[:, None]
        heads = jnp.arange(ssd_w)[None, :] // SSD_HEAD_DIM
        e = (lanes == heads + lane_off).astype(bf16)
        return jnp.concatenate([e, e], axis=0)

    def lane_row(vals, off):
        return jnp.zeros((LANES,), f32).at[off:off + vals.shape[0]].set(vals)

    h = (ctx, x)
    for l in range(depth):
        z, xbc, qk, v, r, small = _inproj(h, mods[l], norm1_g[l][None], w_in_t, l, segments,
                                          conv_w[l].reshape(CONV_K * CONV_K, -1), conv_b[l][None],
                                          widths, bsz, bsz, t, GRID_W)

        pvecs, ees = [], []
        for dirn in range(2):
            off = dirn * ssd_heads
            pvec = jnp.zeros((SUBLANES, LANES), f32)
            pvec = pvec.at[0].set(lane_row(dt_bias[l, dirn], off)).at[1].set(lane_row(a_log[l, dirn], off))
            pvecs.append(pvec.at[2].set(lane_row(jnp.ones((ssd_heads,), f32), off)))
            ees.append(expander(off))
        y_f, y_b = _ssd(xbc, small, pvecs, ees, ssd_w, ssd_heads)

        w2ps = []
        for dirn in range(2):
            off = 2 * ssd_heads + dirn * rank
            w2ps.append(jnp.zeros((LANES, gla_kw), f32).at[off:off + rank].set(gla_w2[l, dirn]).astype(bf16))
        o_f, o_b = _gla(qk, v, small, w2ps, [gla_b2[l, 0][None], gla_b2[l, 1][None]])

        last = l == depth - 1
        dexp = jnp.repeat(d_skip[l, 0] + d_skip[l, 1], SSD_HEAD_DIM)[None]
        h = _outmlp(h, (y_f, y_b, xbc, z, ssd_norm_g[l][None], dexp), (o_f, o_b, r, gla_norm_g[l][None]),
                    mods[l], w_out[l].astype(bf16), norm2_g[l][None],
                    w_ff1[l].astype(bf16), w_ff2[l].astype(bf16), bsz,
                    final_norm_g[None] if last else None)
    return h
```
